```python
import math
import jax
import jax.numpy as jnp
from jax import lax
import numpy as np

D_MODEL = 2048
BATCH = 4
SEQ = 8192
DEPTH = 1

N_META = 16
BLOCK = 128
PAD = BLOCK - N_META

ATT_HEADS = 8
ATT_HD = D_MODEL // (2 * ATT_HEADS)
ATT_VD = 2 * ATT_HD
ATT_QK = ATT_HEADS * 2 * ATT_HD
ATT_V = ATT_HEADS * ATT_VD

HG_HEADS = 16
HG_DK = D_MODEL // HG_HEADS
HG_DV = D_MODEL // HG_HEADS
HG_W = HG_HEADS * HG_DK
HG_WV = HG_HEADS * HG_DV
HG_CHUNK = 64
HG_SUB = 16

REL_BUCKETS = 32
REL_MAX_DIST = 128

N_EXPERTS = 32
TOP_K = 4
D_FF = D_MODEL
SWIGLU_LIMIT = 7.0
SWIGLU_ALPHA = 1.702
MOE_BLOCK = 128

IN_SPLITS = (ATT_QK, ATT_QK, ATT_V, HG_W, HG_W, HG_WV, HG_WV, D_MODEL, D_MODEL)
D_IN = 2 * ATT_QK + ATT_V + 2 * HG_W + 2 * HG_WV + 2 * D_MODEL

RMS_EPS = 1e-6
NEG_INF = -1e30

kernel_name = 'hybrid_diffattn_hgrn2_moe'


def rms_norm(x, gain):
    xf = x.astype(jnp.float32)
    y = xf * lax.rsqrt(jnp.mean(xf * xf, axis=-1, keepdims=True) + RMS_EPS)
    return (y * gain.astype(jnp.float32)).astype(x.dtype)


def pad_seq(t, axis):
    widths = [(0, 0)] * t.ndim
    widths[axis] = (PAD, 0)
    return jnp.pad(t, widths)


def t5_causal_bucket(rel):
    n = jnp.maximum(rel, 0)
    max_exact = REL_BUCKETS // 2
    nf = jnp.maximum(n, 1).astype(jnp.float32)
    large = max_exact + (jnp.log(nf / max_exact) / math.log(REL_MAX_DIST / max_exact)
                         * (REL_BUCKETS - max_exact)).astype(jnp.int32)
    large = jnp.minimum(large, REL_BUCKETS - 1)
    return jnp.where(n < max_exact, n, large)


def diff_attention(q, k, v, lam, rel_bias):
    B, H, _, Lp, d = q.shape
    nqb = Lp // BLOCK
    scale = d ** -0.5
    kpos = jnp.arange(Lp, dtype=jnp.int32)
    key_valid = kpos >= PAD
    q_blocks = jnp.moveaxis(q.reshape(B, H, 2, nqb, BLOCK, d), 3, 0)

    def one_block(args):
        q_blk, start = args
        qpos = start + jnp.arange(BLOCK, dtype=jnp.int32)
        rel = qpos[:, None] - kpos[None, :]
        bias = jnp.moveaxis(rel_bias[t5_causal_bucket(rel)], -1, 0).astype(jnp.float32)
        mask = (rel >= 0) & key_valid[None, :]
        s = jnp.einsum('bhmqd,bhmkd->bhmqk', q_blk, k).astype(jnp.float32) * scale + bias[None, :, None]
        s = jnp.where(mask, s, NEG_INF)
        p = jax.nn.softmax(s, axis=-1)
        a = p[:, :, 0] - lam * p[:, :, 1]
        return jnp.einsum('bhqk,bhkv->bhqv', a.astype(v.dtype), v)

    starts = jnp.arange(nqb, dtype=jnp.int32) * BLOCK
    o = lax.map(one_block, (q_blocks, starts))
    return jnp.moveaxis(o, 0, 2).reshape(B, H, Lp, v.shape[-1])


def hgrn2_chunk_scan(q, k, g, v):
    B, H, Lp, dk = q.shape
    dv = v.shape[-1]
    n_chunks = Lp // HG_CHUNK
    n_sub = HG_CHUNK // HG_SUB
    c = HG_SUB

    def to_chunks(t):
        return jnp.moveaxis(t.reshape(B, H, n_chunks, HG_CHUNK, t.shape[-1]), 2, 0)

    sub_lower = jnp.tril(jnp.ones((n_sub, n_sub), dtype=bool), -1)
    tok_lower = jnp.tril(jnp.ones((c, c), dtype=bool))

    def step(S, xs):
        qc, kc, gc, vc = xs
        b = jnp.cumsum(gc, axis=-2)
        b_end = b[..., -1, :]
        o_inter = jnp.einsum('bhtk,bhkv->bhtv', qc * jnp.exp(b), S)
        S_new = jnp.exp(b_end)[..., None] * S + jnp.einsum(
            'bhsk,bhsv->bhkv', kc * jnp.exp(b_end[..., None, :] - b), vc)
        qs = qc.reshape(B, H, n_sub, c, dk)
        ks = kc.reshape(B, H, n_sub, c, dk)
        bs = b.reshape(B, H, n_sub, c, dk)
        vs = vc.reshape(B, H, n_sub, c, dv)
        e = bs[..., -1, :]
        q_rel = qs[:, :, :, None] * jnp.exp(jnp.minimum(bs[:, :, :, None] - e[:, :, None, :, None, :], 0.0))
        k_rel = ks * jnp.exp(e[:, :, :, None, :] - bs)
        a_off = jnp.einsum('bhijtk,bhjsk->bhijts', q_rel, k_rel)
        a_off = jnp.where(sub_lower[:, :, None, None], a_off, 0.0)
        decay = jnp.exp(jnp.minimum(bs[..., :, None, :] - bs[..., None, :, :], 0.0))
        a_diag = jnp.sum(qs[..., :, None, :] * ks[..., None, :, :] * decay, axis=-1)
        a_diag = jnp.where(tok_lower, a_diag, 0.0)
        o_intra = (jnp.einsum('bhijts,bhjsv->bhitv', a_off, vs)
                   + jnp.einsum('bhits,bhisv->bhitv', a_diag, vs))
        return S_new, o_inter + o_intra.reshape(B, H, HG_CHUNK, dv)

    S0 = jnp.zeros((B, H, dk, dv), jnp.float32)
    _, o = lax.scan(step, S0, (to_chunks(q), to_chunks(k), to_chunks(g), to_chunks(v)))
    return jnp.moveaxis(o, 0, 2).reshape(B, H, Lp, dv)


def token_mixers(u, w_in, q_norm_g, k_norm_g, lam_vecs, lambda_init, subln_g, lb, hg_norm_g, w_out, rel_bias):
    B, L, _ = u.shape
    proj = jnp.einsum('bld,de->ble', u, w_in)
    q, k, v, hq, hf, hi, hog, ga, gh = jnp.split(proj, np.cumsum(IN_SPLITS)[:-1].tolist(), axis=-1)

    q = rms_norm(q.reshape(B, L, ATT_HEADS, 2, ATT_HD), q_norm_g).transpose(0, 2, 3, 1, 4)
    k = rms_norm(k.reshape(B, L, ATT_HEADS, 2, ATT_HD), k_norm_g).transpose(0, 2, 3, 1, 4)
    v = v.reshape(B, L, ATT_HEADS, ATT_VD).transpose(0, 2, 1, 3)
    lv = lam_vecs.astype(jnp.float32)
    lam = jnp.exp(jnp.sum(lv[0] * lv[1])) - jnp.exp(jnp.sum(lv[2] * lv[3])) + lambda_init
    o = diff_attention(pad_seq(q, 3), pad_seq(k, 3), pad_seq(v, 2), lam, rel_bias)[:, :, PAD:]
    o = rms_norm(o, subln_g) * (1.0 - lambda_init)
    o_att = o.transpose(0, 2, 1, 3).reshape(B, L, ATT_V).astype(u.dtype)

    def heads(t, d):
        return t.reshape(B, L, HG_HEADS, d).transpose(0, 2, 1, 3)
    hf32 = hf.astype(jnp.float32)
    lb32 = lb.astype(jnp.float32)
    forget = lb32 + (1.0 - lb32) * jax.nn.sigmoid(hf32)
    g = heads(jnp.log(forget), HG_DK)
    kk = heads((1.0 - lb32) * jax.nn.sigmoid(-hf32), HG_DK)
    qq = heads(jax.nn.silu(hq.astype(jnp.float32)), HG_DK)
    vv = heads(hi.astype(jnp.float32), HG_DV)
    oh = hgrn2_chunk_scan(pad_seq(qq, 2), pad_seq(kk, 2), pad_seq(g, 2), pad_seq(vv, 2))[:, :, PAD:]
    oh = rms_norm(oh, hg_norm_g).transpose(0, 2, 1, 3).reshape(B, L, HG_WV)
    o_hg = oh.astype(u.dtype) * jax.nn.silu(hog)

    y = jax.nn.sigmoid(ga) * o_att + jax.nn.sigmoid(gh) * o_hg
    return jnp.einsum('ble,ed->bld', y, w_out)


def moe(h, router_w, router_b, w_gu, b_gu, w_dn, b_dn):
    B, L, D = h.shape
    T = B * L
    ht = h.reshape(T, D)
    logits = (ht @ router_w + router_b).astype(jnp.float32)
    top_val, top_idx = lax.top_k(logits, TOP_K)
    gate = jax.nn.softmax(top_val, axis=-1)
    A = T * TOP_K
    exp_flat = top_idx.reshape(A)
    tok_flat = jnp.arange(A, dtype=jnp.int32) // TOP_K
    gate_flat = gate.reshape(A)
    order = jnp.argsort(exp_flat)
    exp_sorted = exp_flat[order]
    counts = jnp.bincount(exp_flat, length=N_EXPERTS)
    padded = (counts + MOE_BLOCK - 1) // MOE_BLOCK * MOE_BLOCK
    pend = jnp.cumsum(padded)
    pstart = pend - padded
    ustart = jnp.cumsum(counts) - counts
    dest = pstart[exp_sorted] + (jnp.arange(A, dtype=jnp.int32) - ustart[exp_sorted])
    n_blocks = -(-A // MOE_BLOCK) + N_EXPERTS
    R = n_blocks * MOE_BLOCK
    slot_tok = jnp.full((R,), T, jnp.int32).at[dest].set(tok_flat[order])
    slot_gate = jnp.zeros((R,), jnp.float32).at[dest].set(gate_flat[order])
    block_exp = jnp.minimum(jnp.searchsorted(pend, jnp.arange(n_blocks) * MOE_BLOCK, side='right'), N_EXPERTS - 1)
    h_ext = jnp.concatenate([ht, jnp.zeros((1, D), ht.dtype)], axis=0)

    def expert_block(args):
        toks, e = args
        xb = h_ext[toks]
        gu = xb @ w_gu[e] + b_gu[e]
        x_glu = jnp.minimum(gu[:, :D_FF], SWIGLU_LIMIT)
        x_lin = jnp.clip(gu[:, D_FF:], -SWIGLU_LIMIT, SWIGLU_LIMIT)
        act = x_glu * jax.nn.sigmoid(SWIGLU_ALPHA * x_glu) * (x_lin + 1.0)
        return act @ w_dn[e] + b_dn[e]

    y = lax.map(expert_block, (slot_tok.reshape(n_blocks, MOE_BLOCK), block_exp))
    y = y.reshape(R, D) * slot_gate[:, None].astype(y.dtype)
    out = jnp.zeros((T + 1, D), y.dtype).at[slot_tok].add(y)[:T]
    return out.reshape(B, L, D)


def setup_inputs(seed: int = 0) -> dict:
    key = jax.random.key(seed)
    ks = jax.random.split(key, 20)
    f32 = jnp.float32

    def nrm(k, shape, scale):
        return jax.random.normal(k, shape, f32) * scale

    return {
        'x': nrm(ks[0], (BATCH, SEQ, D_MODEL), 1.0),
        'meta_tokens': nrm(ks[1], (N_META, D_MODEL), 1.0),
        'rel_bias': nrm(ks[2], (REL_BUCKETS, ATT_HEADS), 0.5),
        'lb_logits': 1.0 + nrm(ks[3], (DEPTH + 1, HG_W), 0.1),
        'norm1': 1.0 + nrm(ks[4], (DEPTH, D_MODEL), 0.05),
        'w_in': nrm(ks[5], (DEPTH, D_MODEL, D_IN), D_MODEL ** -0.5),
        'q_norm': 1.0 + nrm(ks[6], (DEPTH, ATT_HD), 0.05),
        'k_norm': 1.0 + nrm(ks[7], (DEPTH, ATT_HD), 0.05),
        'diff_lambda': nrm(ks[8], (DEPTH, 4, ATT_HD), 0.1),
        'diff_subln': 1.0 + nrm(ks[9], (DEPTH, ATT_VD), 0.05),
        'hgrn_norm': 1.0 + nrm(ks[10], (DEPTH, HG_DV), 0.05),
        'w_out': nrm(ks[11], (DEPTH, D_MODEL, D_MODEL), D_MODEL ** -0.5),
        'norm2': 1.0 + nrm(ks[12], (DEPTH, D_MODEL), 0.05),
        'router_w': nrm(ks[13], (DEPTH, D_MODEL, N_EXPERTS), D_MODEL ** -0.5),
        'router_b': nrm(ks[14], (DEPTH, N_EXPERTS), 0.01),
        'w_gate_up': nrm(ks[15], (DEPTH, N_EXPERTS, D_MODEL, 2 * D_FF), D_MODEL ** -0.5),
        'b_gate_up': nrm(ks[16], (DEPTH, N_EXPERTS, 2 * D_FF), 0.01),
        'w_down': nrm(ks[17], (DEPTH, N_EXPERTS, D_FF, D_MODEL), D_FF ** -0.5),
        'b_down': nrm(ks[18], (DEPTH, N_EXPERTS, D_MODEL), 0.01),
    }


def reference(x, meta_tokens, rel_bias, lb_logits, norm1, w_in, q_norm, k_norm, diff_lambda, diff_subln,
              hgrn_norm, w_out, norm2, router_w, router_b, w_gate_up, b_gate_up, w_down, b_down):
    B = x.shape[0]
    meta = jnp.broadcast_to(meta_tokens[None].astype(x.dtype), (B, N_META, D_MODEL))
    h = jnp.concatenate([meta, x], axis=1)
    lower_bounds = jnp.cumsum(jax.nn.softmax(lb_logits.astype(jnp.float32), axis=0), axis=0)
    for l in range(DEPTH):
        lambda_init = 0.8 - 0.6 * math.exp(-0.3 * l)
        h = h + token_mixers(rms_norm(h, norm1[l]), w_in[l], q_norm[l], k_norm[l], diff_lambda[l],
                             lambda_init, diff_subln[l], lower_bounds[l], hgrn_norm[l], w_out[l], rel_bias)
        h = h + moe(rms_norm(h, norm2[l]), router_w[l], router_b[l], w_gate_up[l], b_gate_up[l],
                    w_down[l], b_down[l])
    return h[:, N_META:]
```

```python
import functools
import math

import jax
import jax.numpy as jnp
from jax import lax
from jax.experimental import pallas as pl
from jax.experimental.pallas import tpu as pltpu

N_META = 16
BLOCK = 128
PAD = BLOCK - N_META

ATT_HEADS = 8
ATT_HD = 128
ATT_VD = 2 * ATT_HD

HG_HEADS = 16
HG_DK = 128
HG_CHUNK = 64
HG_SUB = 16

REL_BUCKETS = 32
REL_MAX_DIST = 128

N_EXPERTS = 32
TOP_K = 4
SWIGLU_LIMIT = 7.0
SWIGLU_ALPHA = 1.702

RMS_EPS = 1e-6
NEG_INF = -1e30
LAMBDA_INIT = 0.8 - 0.6 * math.exp(-0.3 * 0)

N_SEG = 9
LANES = 128
SUBLANES = 8
ROW_TILE = 16
VMEM_LIMIT = 56 * 1024 * 1024

F32 = jnp.float32
BF16 = jnp.bfloat16


def _largest_divisor(n, candidates):
    for c in candidates:
        if n % c == 0:
            return c
    raise ValueError(f"no tile in {candidates} divides {n}")


def _sigmoid(x):
    return 1.0 / (1.0 + jnp.exp(-x))


def _inproj_kernel(x_ref, n1_ref, w_ref, cp_ref, main_ref, g_ref, u_ref, *, nsub, tn):
    j = pl.program_id(1)

    @pl.when(j == 0)
    def _():
        xf = x_ref[...]
        ms = jnp.mean(xf * xf, axis=-1, keepdims=True)
        u_ref[...] = (xf * lax.rsqrt(ms + RMS_EPS) * n1_ref[...]).astype(BF16)

    acc = jnp.dot(u_ref[...], w_ref[...], preferred_element_type=F32)
    seg = j // nsub

    @pl.when(seg <= 1)
    def _():
        for gi in range(tn // LANES):
            sl = slice(gi * LANES, (gi + 1) * LANES)
            y = acc[:, sl]
            ms = jnp.mean(y * y, axis=-1, keepdims=True)
            main_ref[0, :, sl] = (y * lax.rsqrt(ms + RMS_EPS) * cp_ref[0:1, sl]).astype(BF16)

    @pl.when((seg == 2) | (seg == 5))
    def _():
        main_ref[0] = acc.astype(BF16)

    @pl.when((seg == 3) | (seg == 6))
    def _():
        main_ref[0] = (acc * _sigmoid(acc)).astype(BF16)

    @pl.when(seg == 4)
    def _():
        lb = cp_ref[1:2, :]
        sg = _sigmoid(acc)
        g_ref[...] = jnp.log(lb + (1.0 - lb) * sg)
        main_ref[0] = ((1.0 - lb) * (1.0 - sg)).astype(BF16)

    @pl.when(seg >= 7)
    def _():
        main_ref[0] = _sigmoid(acc).astype(BF16)


def _in_proj(h_pad, n1, w_in_bf, colp):
    rp, d = h_pad.shape
    tm = _largest_divisor(rp, (512, 256, 128))
    tn = 1024
    nsub = d // tn
    nj = N_SEG * nsub
    kern = functools.partial(_inproj_kernel, nsub=nsub, tn=tn)
    return pl.pallas_call(
        kern,
        grid=(rp // tm, nj),
        in_specs=[
            pl.BlockSpec((tm, d), lambda i, j: (i, 0)),
            pl.BlockSpec((1, d), lambda i, j: (0, 0)),
            pl.BlockSpec((d, tn), lambda i, j: (0, j)),
            pl.BlockSpec((2, tn), lambda i, j: (0, j)),
        ],
        out_specs=[
            pl.BlockSpec((1, tm, tn), lambda i, j: (j // nsub, i, j % nsub)),
            pl.BlockSpec((tm, tn), lambda i, j: (i, jnp.clip(j - 4 * nsub, 0, nsub - 1))),
        ],
        out_shape=[
            jax.ShapeDtypeStruct((N_SEG, rp, d), BF16),
            jax.ShapeDtypeStruct((rp, d), F32),
        ],
        scratch_shapes=[pltpu.VMEM((tm, d), BF16)],
        compiler_params=pltpu.CompilerParams(
            dimension_semantics=("arbitrary", "arbitrary"), vmem_limit_bytes=VMEM_LIMIT),
        name="in_proj",
    )(h_pad, n1, w_in_bf, colp)


def _t5_bucket(rel):
    n = jnp.maximum(rel, 0)
    max_exact = REL_BUCKETS // 2
    nf = jnp.maximum(n, 1).astype(F32)
    large = max_exact + (jnp.log(nf / max_exact) / math.log(REL_MAX_DIST / max_exact)
                         * (REL_BUCKETS - max_exact)).astype(jnp.int32)
    large = jnp.minimum(large, REL_BUCKETS - 1)
    return jnp.where(n < max_exact, n, large)


def _bias_tiles(rel_bias, t):
    rb = rel_bias.astype(F32) - rel_bias.astype(F32)[REL_BUCKETS - 1][None, :]
    r = jnp.arange(t, dtype=jnp.int32)[:, None]
    c = jnp.arange(t, dtype=jnp.int32)[None, :]
    rel_d = r - c
    bd = jnp.where((rel_d >= 0)[..., None], rb[_t5_bucket(rel_d)], NEG_INF)
    bp = rb[_t5_bucket(rel_d + t)]
    return jnp.moveaxis(bd, -1, 0), jnp.moveaxis(bp, -1, 0)


def _attn_kernel(qi_tab, ki_tab, lam_ref, q_ref, k_ref, v_ref, bd_ref, bp_ref, sg_ref, o_ref,
                 m_sc, l_sc, acc_sc, *, bq, bk):
    p = pl.program_id(2)
    qi = qi_tab[p]
    ki = ki_tab[p]

    @pl.when(ki == 0)
    def _():
        m_sc[...] = jnp.full(m_sc.shape, NEG_INF, F32)
        l_sc[...] = jnp.zeros(l_sc.shape, F32)
        acc_sc[...] = jnp.zeros(acc_sc.shape, F32)

    def scores(m):
        sl = slice(m * ATT_HD, (m + 1) * ATT_HD)
        return lax.dot_general(q_ref[0, :, sl], k_ref[0, :, sl], (((1,), (1,)), ((), ())),
                               preferred_element_type=F32)

    def update(m, s):
        m_old = m_sc[m]
        m_new = jnp.maximum(m_old, jnp.max(s, axis=-1, keepdims=True))
        alpha = jnp.exp(m_old - m_new)
        pm = jnp.exp(s - m_new)
        l_sc[m] = alpha * l_sc[m] + jnp.sum(pm, axis=-1, keepdims=True)
        acc_sc[m] = alpha * acc_sc[m] + jnp.dot(pm.astype(BF16), v_ref[0], preferred_element_type=F32)
        m_sc[m] = m_new

    def key_valid():
        col = ki * bk + lax.broadcasted_iota(jnp.int32, (1, bk), 1)
        return col >= PAD

    @pl.when(ki == qi)
    def _():
        ok = key_valid()
        for m in range(2):
            update(m, jnp.where(ok, scores(m) + bd_ref[0], NEG_INF))
        lam = lam_ref[0, 0]
        o = acc_sc[0] / l_sc[0] - lam * (acc_sc[1] / l_sc[1])
        ms = jnp.mean(o * o, axis=-1, keepdims=True)
        o_ref[...] = (o * lax.rsqrt(ms + RMS_EPS) * sg_ref[...]).astype(BF16)

    @pl.when(ki == qi - 1)
    def _():
        ok = key_valid()
        for m in range(2):
            update(m, jnp.where(ok, scores(m) + bp_ref[0], NEG_INF))

    @pl.when((ki < qi - 1) & (ki == 0))
    def _():
        ok = key_valid()
        for m in range(2):
            update(m, jnp.where(ok, scores(m), NEG_INF))

    @pl.when((ki < qi - 1) & (ki > 0))
    def _():
        for m in range(2):
            update(m, scores(m))


def _diff_attention(main, lam, bias_d, bias_p, subln, batch, lp, blk):
    _, rp, d = main.shape
    nb = lp // blk
    pairs = [(qi, ki) for qi in range(nb) for ki in range(qi + 1)]
    qi_tab = jnp.asarray([a for a, _ in pairs], jnp.int32)
    ki_tab = jnp.asarray([b for _, b in pairs], jnp.int32)
    kern = functools.partial(_attn_kernel, bq=blk, bk=blk)
    grid_spec = pltpu.PrefetchScalarGridSpec(
        num_scalar_prefetch=2,
        grid=(batch, ATT_HEADS, len(pairs)),
        in_specs=[
            pl.BlockSpec(memory_space=pltpu.SMEM),
            pl.BlockSpec((1, blk, ATT_VD), lambda b, h, p, qt, kt: (0, b * nb + qt[p], h)),
            pl.BlockSpec((1, blk, ATT_VD), lambda b, h, p, qt, kt: (1, b * nb + kt[p], h)),
            pl.BlockSpec((1, blk, ATT_VD), lambda b, h, p, qt, kt: (2, b * nb + kt[p], h)),
            pl.BlockSpec((1, blk, blk), lambda b, h, p, qt, kt: (h, 0, 0)),
            pl.BlockSpec((1, blk, blk), lambda b, h, p, qt, kt: (h, 0, 0)),
            pl.BlockSpec((1, ATT_VD), lambda b, h, p, qt, kt: (0, 0)),
        ],
        out_specs=pl.BlockSpec((blk, ATT_VD), lambda b, h, p, qt, kt: (b * nb + qt[p], h)),
        scratch_shapes=[
            pltpu.VMEM((2, blk, 1), F32),
            pltpu.VMEM((2, blk, 1), F32),
            pltpu.VMEM((2, blk, ATT_VD), F32),
        ],
    )
    return pl.pallas_call(
        kern,
        grid_spec=grid_spec,
        out_shape=jax.ShapeDtypeStruct((rp, d), BF16),
        compiler_params=pltpu.CompilerParams(
            dimension_semantics=("arbitrary", "arbitrary", "arbitrary"), vmem_limit_bytes=VMEM_LIMIT),
        name="diff_attn",
    )(qi_tab, ki_tab, lam, main, main, main, bias_d, bias_p, subln)


def _hgrn_kernel(q_ref, k_ref, v_ref, g_ref, gn_ref, o_ref, st_sc, b_sc, kf_sc, vf_sc, *, tc):
    c_len = HG_CHUNK
    n_sub = HG_CHUNK // HG_SUB
    half = SUBLANES

    @pl.when(pl.program_id(2) == 0)
    def _():
        st_sc[...] = jnp.zeros(st_sc.shape, F32)

    ri = lax.broadcasted_iota(jnp.int32, (c_len, c_len), 0)
    ci = lax.broadcasted_iota(jnp.int32, (c_len, c_len), 1)
    tri = (ri >= ci).astype(F32)
    sub_r = ri // HG_SUB
    sub_c = ci // HG_SUB
    r2 = lax.broadcasted_iota(jnp.int32, (2 * LANES, 2 * LANES), 0) // LANES
    c2 = lax.broadcasted_iota(jnp.int32, (2 * LANES, 2 * LANES), 1) // LANES
    ones_bd = (r2 == c2).astype(BF16)
    row8 = lax.broadcasted_iota(jnp.int32, (half, LANES), 0)
    nt = (((1,), (1,)), ((), ()))

    def chunk(c, carry):
        r0 = pl.multiple_of(c * c_len, c_len)
        q = q_ref[0, pl.ds(r0, c_len), :].astype(F32)
        k = k_ref[0, pl.ds(r0, c_len), :].astype(F32)
        v_bf = v_ref[0, pl.ds(r0, c_len), :]
        g = g_ref[pl.ds(r0, c_len), :]
        b = jnp.dot(tri, g, precision=lax.Precision.HIGHEST, preferred_element_type=F32)
        b_sc[...] = b
        kf_sc[...] = k
        vf_sc[...] = v_bf.astype(F32)
        b_end = b[c_len - 1:c_len, :]
        st = st_sc[...]

        o = lax.dot_general((q * jnp.exp(b)).astype(BF16), st.astype(BF16), nt, preferred_element_type=F32)

        e_rows = [b[(j + 1) * HG_SUB - 1:(j + 1) * HG_SUB, :] for j in range(n_sub)]
        e_full = jnp.concatenate([jnp.broadcast_to(e, (HG_SUB, LANES)) for e in e_rows], axis=0)
        k_rel = (k * jnp.exp(e_full - b)).astype(BF16)
        q_rel = jnp.concatenate(
            [(q * jnp.exp(jnp.minimum(b - e_rows[j], 0.0))).astype(BF16) for j in range(n_sub - 1)], axis=0)
        a_all = lax.dot_general(q_rel, k_rel, nt, preferred_element_type=F32)
        a_off = jnp.zeros((c_len, c_len), F32)
        for j in range(n_sub - 1):
            a_off = jnp.where(sub_c == j, a_all[j * c_len:(j + 1) * c_len], a_off)
        a_off = jnp.where(sub_r > sub_c, a_off, 0.0)
        o = o + jnp.dot(a_off.astype(BF16), v_bf, preferred_element_type=F32)

        pieces = []
        meta = []
        for i in range(n_sub):
            for s in range(HG_SUB):
                row = i * HG_SUB + s
                ks = kf_sc[pl.ds(row, 1), :]
                bs = b_sc[pl.ds(row, 1), :]
                for hf in range(HG_SUB // half):
                    t0 = hf * half
                    if t0 + half - 1 < s:
                        continue
                    rows = slice(i * HG_SUB + t0, i * HG_SUB + t0 + half)
                    w = q[rows] * ks * jnp.exp(jnp.minimum(b[rows] - bs, 0.0))
                    if s > t0:
                        w = jnp.where(row8 + t0 >= s, w, 0.0)
                    pieces.append(w.astype(BF16))
                    meta.append((i, s, hf))
        n_pairs = len(pieces) // 2
        lhs = jnp.concatenate(
            [jnp.concatenate([pieces[2 * n], pieces[2 * n + 1]], axis=1) for n in range(n_pairs)], axis=0)
        sums = jnp.dot(lhs, ones_bd, preferred_element_type=F32)
        diag = [[jnp.zeros((half, LANES), F32) for _ in range(HG_SUB // half)] for _ in range(n_sub)]
        for n, (i, s, hf) in enumerate(meta):
            blk = sums[(n // 2) * half:(n // 2 + 1) * half, (n % 2) * LANES:(n % 2 + 1) * LANES]
            vs = vf_sc[pl.ds(i * HG_SUB + s, 1), :]
            diag[i][hf] = diag[i][hf] + blk * vs
        o = o + jnp.concatenate([d for row in diag for d in row], axis=0)

        k_out = (k * jnp.exp(b_end - b)).astype(BF16)
        st_sc[...] = st * jnp.exp(b_end) + lax.dot_general(
            v_bf, k_out, (((0,), (0,)), ((), ())), preferred_element_type=F32)

        ms = jnp.mean(o * o, axis=-1, keepdims=True)
        o_ref[pl.ds(r0, c_len), :] = (o * lax.rsqrt(ms + RMS_EPS) * gn_ref[...]).astype(BF16)
        return carry

    lax.fori_loop(0, tc // c_len, chunk, 0)


def _hgrn2(main, g, gn, batch, lp):
    _, rp, d = main.shape
    tc = _largest_divisor(lp, (640, 512, 256, 128, 64))
    nt_ = lp // tc
    kern = functools.partial(_hgrn_kernel, tc=tc)

    def seg_spec(seg):
        return pl.BlockSpec((1, tc, HG_DK), lambda b, h, t: (seg, b * nt_ + t, h))

    return pl.pallas_call(
        kern,
        grid=(batch, HG_HEADS, nt_),
        in_specs=[
            seg_spec(3), seg_spec(4), seg_spec(5),
            pl.BlockSpec((tc, HG_DK), lambda b, h, t: (b * nt_ + t, h)),
            pl.BlockSpec((1, HG_DK), lambda b, h, t: (0, 0)),
        ],
        out_specs=pl.BlockSpec((tc, HG_DK), lambda b, h, t: (b * nt_ + t, h)),
        out_shape=jax.ShapeDtypeStruct((rp, d), BF16),
        scratch_shapes=[
            pltpu.VMEM((HG_DK, HG_DK), F32),
            pltpu.VMEM((HG_CHUNK, HG_DK), F32),
            pltpu.VMEM((HG_CHUNK, HG_DK), F32),
            pltpu.VMEM((HG_CHUNK, HG_DK), F32),
        ],
        compiler_params=pltpu.CompilerParams(
            dimension_semantics=("arbitrary", "arbitrary", "arbitrary"), vmem_limit_bytes=VMEM_LIMIT),
        name="hgrn2",
    )(main, main, main, g, gn)


def _outproj_kernel(ga_ref, gh_ref, og_ref, oa_ref, oh_ref, h_ref, w_ref, n2_ref, rw_ref, rb_ref,
                    h1_ref, u2_ref, ti_ref, gt_ref):
    y = (ga_ref[0].astype(F32) * oa_ref[...].astype(F32)
         + gh_ref[0].astype(F32) * (oh_ref[...].astype(F32) * og_ref[0].astype(F32)))
    h1 = h_ref[...] + jnp.dot(y.astype(BF16), w_ref[...], preferred_element_type=F32)
    h1_ref[...] = h1
    ms = jnp.mean(h1 * h1, axis=-1, keepdims=True)
    u2 = h1 * lax.rsqrt(ms + RMS_EPS) * n2_ref[...]
    u2_ref[...] = u2
    logits = jnp.dot(u2, rw_ref[...], precision=lax.Precision.HIGHEST,
                     preferred_element_type=F32) + rb_ref[...]
    lane = lax.broadcasted_iota(jnp.int32, logits.shape, 1)
    cur = logits
    vals, idxs = [], []
    for _ in range(TOP_K):
        mx = jnp.max(cur, axis=-1, keepdims=True)
        ix = jnp.min(jnp.where(cur == mx, lane, LANES), axis=-1, keepdims=True)
        vals.append(mx)
        idxs.append(ix)
        cur = jnp.where(lane == ix, -jnp.inf, cur)
    es = [jnp.exp(v - vals[0]) for v in vals]
    inv = 1.0 / (es[0] + es[1] + es[2] + es[3])
    ti = jnp.zeros(logits.shape, jnp.int32)
    gt = jnp.zeros(logits.shape, F32)
    for kk in range(TOP_K):
        ti = jnp.where(lane == kk, idxs[kk], ti)
        gt = jnp.where(lane == kk, es[kk] * inv, gt)
    ti_ref[...] = ti
    gt_ref[...] = gt


def _out_proj(main, o_att, o_hg, h_pad, w_out_bf, n2, rw_pad, rb_pad):
    rp, d = h_pad.shape
    tm = _largest_divisor(rp, (256, 128))

    def seg_spec(seg):
        return pl.BlockSpec((1, tm, d), lambda i: (seg, i, 0))

    row = pl.BlockSpec((tm, d), lambda i: (i, 0))
    small = pl.BlockSpec((tm, LANES), lambda i: (i, 0))
    return pl.pallas_call(
        _outproj_kernel,
        grid=(rp // tm,),
        in_specs=[
            seg_spec(7), seg_spec(8), seg_spec(6), row, row, row,
            pl.BlockSpec((d, d), lambda i: (0, 0)),
            pl.BlockSpec((1, d), lambda i: (0, 0)),
            pl.BlockSpec((d, LANES), lambda i: (0, 0)),
            pl.BlockSpec((1, LANES), lambda i: (0, 0)),
        ],
        out_specs=[row, row, small, small],
        out_shape=[
            jax.ShapeDtypeStruct((rp, d), F32),
            jax.ShapeDtypeStruct((rp, d), F32),
            jax.ShapeDtypeStruct((rp, LANES), jnp.int32),
            jax.ShapeDtypeStruct((rp, LANES), F32),
        ],
        compiler_params=pltpu.CompilerParams(
            dimension_semantics=("arbitrary",), vmem_limit_bytes=VMEM_LIMIT),
        name="out_proj",
    )(main, main, main, o_att, o_hg, h_pad, w_out_bf, n2, rw_pad, rb_pad)


def _rank_kernel(ti_ref, rank_ref, cnt_ref, carry_sc, *, tm, lp):
    i = pl.program_id(0)

    @pl.when(i == 0)
    def _():
        carry_sc[...] = jnp.zeros(carry_sc.shape, F32)

    ti = ti_ref[...]
    lane = lax.broadcasted_iota(jnp.int32, (tm, LANES), 1)
    row = i * tm + lax.broadcasted_iota(jnp.int32, (tm, 1), 0)
    valid = lax.rem(row, lp) >= PAD
    hot = [(lane == ti[:, kk:kk + 1]) & valid for kk in range(TOP_K)]
    any_hot = hot[0] | hot[1] | hot[2] | hot[3]
    any_f = jnp.where(any_hot, 1.0, 0.0)
    rr = lax.broadcasted_iota(jnp.int32, (tm, tm), 0)
    cc = lax.broadcasted_iota(jnp.int32, (tm, tm), 1)
    strict = jnp.where(rr > cc, 1.0, 0.0).astype(BF16)
    base = carry_sc[...] + jnp.dot(strict, any_f.astype(BF16), preferred_element_type=F32)
    rank = jnp.zeros((tm, LANES), F32)
    for kk in range(TOP_K):
        rk = jnp.sum(jnp.where(hot[kk], base, 0.0), axis=-1, keepdims=True)
        rank = jnp.where(lane == kk, rk, rank)
    rank_ref[...] = rank.astype(jnp.int32)
    carry_sc[...] = carry_sc[...] + jnp.sum(any_f, axis=0, keepdims=True)
    cnt_ref[...] = carry_sc[...]


def _moe_rank(topi, lp):
    rp = topi.shape[0]
    tm = _largest_divisor(rp, (256, 128))
    kern = functools.partial(_rank_kernel, tm=tm, lp=lp)
    return pl.pallas_call(
        kern,
        grid=(rp // tm,),
        in_specs=[pl.BlockSpec((tm, LANES), lambda i: (i, 0))],
        out_specs=[pl.BlockSpec((tm, LANES), lambda i: (i, 0)), pl.BlockSpec((1, LANES), lambda i: (0, 0))],
        out_shape=[jax.ShapeDtypeStruct((rp, LANES), jnp.int32), jax.ShapeDtypeStruct((1, LANES), F32)],
        scratch_shapes=[pltpu.VMEM((1, LANES), F32)],
        compiler_params=pltpu.CompilerParams(dimension_semantics=("arbitrary",)),
        name="moe_rank",
    )(topi)


def _row_gather_start(src_hbm, dst, sem, tok, r):
    pltpu.make_async_copy(
        src_hbm.at[pl.ds(pl.multiple_of(tok * ROW_TILE, ROW_TILE), ROW_TILE), :],
        dst.at[pl.ds(pl.multiple_of(r * ROW_TILE, ROW_TILE), ROW_TILE), :],
        sem).start()


def _ffn_kernel(bexp_ref, nact_ref, tokc_ref, tokn_ref, u2_hbm, wg_ref, wl_ref, bg_ref, bl_ref, wd_ref,
                bd_ref, ys_ref, gbuf, xb, acc, sem, *, rb, nff, d):
    i = pl.program_id(0)
    f = pl.program_id(1)
    nact = nact_ref[0]
    slot = lax.rem(i, 2)
    row_w = d // ROW_TILE

    def issue(tok_ref, sl):
        def body(r, carry):
            _row_gather_start(u2_hbm, gbuf.at[sl], sem.at[sl], tok_ref[0, 0, r], r)
            return carry
        lax.fori_loop(0, rb, body, 0, unroll=8)

    @pl.when((f == 0) & (i == 0))
    def _():
        issue(tokc_ref, 0)

    @pl.when((f == 0) & (i + 1 < nact))
    def _():
        issue(tokn_ref, 1 - slot)

    @pl.when((f == 0) & (i < nact))
    def _():
        pltpu.make_async_copy(gbuf.at[slot], gbuf.at[slot], sem.at[slot]).wait()
        for s in range(ROW_TILE):
            xb[:, s * row_w:(s + 1) * row_w] = gbuf[slot, pl.ds(s, rb, stride=ROW_TILE), :].astype(BF16)
        acc[...] = jnp.broadcast_to(bd_ref[0], acc.shape)

    @pl.when(i < nact)
    def _():
        x = xb[...]
        xg = jnp.dot(x, wg_ref[0].astype(BF16), preferred_element_type=F32) + bg_ref[0]
        xl = jnp.dot(x, wl_ref[0].astype(BF16), preferred_element_type=F32) + bl_ref[0]
        xg = jnp.minimum(xg, SWIGLU_LIMIT)
        xl = jnp.clip(xl, -SWIGLU_LIMIT, SWIGLU_LIMIT)
        act = xg * _sigmoid(SWIGLU_ALPHA * xg) * (xl + 1.0)
        acc[...] += jnp.dot(act.astype(BF16), wd_ref[0].astype(BF16), preferred_element_type=F32)

    @pl.when((f == nff - 1) & (i < nact))
    def _():
        for s in range(ROW_TILE):
            ys_ref[pl.ds(s, rb, stride=ROW_TILE), :] = acc[:, s * row_w:(s + 1) * row_w]

    @pl.when((f == nff - 1) & (i >= nact))
    def _():
        ys_ref[...] = jnp.zeros(ys_ref.shape, F32)


def _moe_ffn(block_exp, nact, slot_tok, u2_rows, w_gu, b_gu, w_dn, b_dn, rb):
    n_e, d, f2 = w_gu.shape
    dff = f2 // 2
    nblk = slot_tok.shape[0]
    tff = 256
    nff = dff // tff
    row_w = d // ROW_TILE
    kern = functools.partial(_ffn_kernel, rb=rb, nff=nff, d=d)

    def f_eff(i, f, na):
        return jnp.where(i < na[0], f, nff - 1)

    grid_spec = pltpu.PrefetchScalarGridSpec(
        num_scalar_prefetch=2,
        grid=(nblk, nff),
        in_specs=[
            pl.BlockSpec((1, 1, rb), lambda i, f, be, na: (i, 0, 0), memory_space=pltpu.SMEM),
            pl.BlockSpec((1, 1, rb), lambda i, f, be, na: (jnp.minimum(i + 1, nblk - 1), 0, 0),
                         memory_space=pltpu.SMEM),
            pl.BlockSpec(memory_space=pl.ANY),
            pl.BlockSpec((1, d, tff), lambda i, f, be, na: (be[i], 0, f_eff(i, f, na))),
            pl.BlockSpec((1, d, tff), lambda i, f, be, na: (be[i], 0, nff + f_eff(i, f, na))),
            pl.BlockSpec((1, 1, tff), lambda i, f, be, na: (be[i], 0, f_eff(i, f, na))),
            pl.BlockSpec((1, 1, tff), lambda i, f, be, na: (be[i], 0, nff + f_eff(i, f, na))),
            pl.BlockSpec((1, tff, d), lambda i, f, be, na: (be[i], f_eff(i, f, na), 0)),
            pl.BlockSpec((1, 1, d), lambda i, f, be, na: (be[i], 0, 0)),
        ],
        out_specs=pl.BlockSpec((rb * ROW_TILE, row_w), lambda i, f, be, na: (i, 0)),
        scratch_shapes=[
            pltpu.VMEM((2, rb * ROW_TILE, row_w), F32),
            pltpu.VMEM((rb, d), BF16),
            pltpu.VMEM((rb, d), F32),
            pltpu.SemaphoreType.DMA((2,)),
        ],
    )
    return pl.pallas_call(
        kern,
        grid_spec=grid_spec,
        out_shape=jax.ShapeDtypeStruct((nblk * rb * ROW_TILE, row_w), F32),
        compiler_params=pltpu.CompilerParams(
            dimension_semantics=("arbitrary", "arbitrary"), vmem_limit_bytes=VMEM_LIMIT),
        name="moe_ffn",
    )(block_exp, nact, slot_tok, slot_tok, u2_rows, w_gu, w_gu, b_gu.reshape(n_e, 1, f2),
      b_gu.reshape(n_e, 1, f2), w_dn, b_dn.reshape(n_e, 1, d))


def _combine_kernel(dc_ref, dn_ref, ys_hbm, gt_ref, h1_ref, o_ref, cbuf, sem, *, tm, d, n_steps, n_i):
    n = pl.program_id(0) * n_i + pl.program_id(1)
    slot = lax.rem(n, 2)
    row_w = d // ROW_TILE

    def issue(dref, sl):
        def body(r, carry):
            for kk in range(TOP_K):
                _row_gather_start(ys_hbm, cbuf.at[sl, kk], sem.at[sl], dref[0, 0, r * TOP_K + kk], r)
            return carry
        lax.fori_loop(0, tm, body, 0, unroll=4)

    @pl.when(n == 0)
    def _():
        issue(dc_ref, 0)

    @pl.when(n + 1 < n_steps)
    def _():
        issue(dn_ref, 1 - slot)

    pltpu.make_async_copy(cbuf.at[slot], cbuf.at[slot], sem.at[slot]).wait()
    gates = [jnp.broadcast_to(gt_ref[:, kk:kk + 1], (tm, row_w)) for kk in range(TOP_K)]
    for s in range(ROW_TILE):
        sl = slice(s * row_w, (s + 1) * row_w)
        y = h1_ref[:, sl]
        for kk in range(TOP_K):
            y = y + gates[kk] * cbuf[slot, kk, pl.ds(s, tm, stride=ROW_TILE), :]
        o_ref[0, :, sl] = y


def _combine(dest_x, ys_rows, gate, h1, batch, seq, lp):
    rp, d = h1.shape
    tm = BLOCK
    n_i = seq // tm
    n_steps = batch * n_i
    nlp = lp // tm
    row_w = d // ROW_TILE
    kern = functools.partial(_combine_kernel, tm=tm, d=d, n_steps=n_steps, n_i=n_i)
    return pl.pallas_call(
        kern,
        grid=(batch, n_i),
        in_specs=[
            pl.BlockSpec((1, 1, tm * TOP_K), lambda b, i: (b * n_i + i, 0, 0), memory_space=pltpu.SMEM),
            pl.BlockSpec((1, 1, tm * TOP_K), lambda b, i: (jnp.minimum(b * n_i + i + 1, n_steps - 1), 0, 0),
                         memory_space=pltpu.SMEM),
            pl.BlockSpec(memory_space=pl.ANY),
            pl.BlockSpec((tm, LANES), lambda b, i: (b * nlp + 1 + i, 0)),
            pl.BlockSpec((tm, d), lambda b, i: (b * nlp + 1 + i, 0)),
        ],
        out_specs=pl.BlockSpec((1, tm, d), lambda b, i: (b, i, 0)),
        out_shape=jax.ShapeDtypeStruct((batch, seq, d), F32),
        scratch_shapes=[
            pltpu.VMEM((2, TOP_K, tm * ROW_TILE, row_w), F32),
            pltpu.SemaphoreType.DMA((2,)),
        ],
        compiler_params=pltpu.CompilerParams(
            dimension_semantics=("arbitrary", "arbitrary"), vmem_limit_bytes=VMEM_LIMIT),
        name="moe_combine",
    )(dest_x, dest_x, ys_rows, gate, h1)


def kernel(x, meta_tokens, rel_bias, lb_logits, norm1, w_in, q_norm, k_norm, diff_lambda, diff_subln,
           hgrn_norm, w_out, norm2, router_w, router_b, w_gate_up, b_gate_up, w_down, b_down):
    batch, seq, d = x.shape
    assert norm1.shape[0] == 1, "single-layer block"
    assert d == ATT_HEADS * ATT_VD == HG_HEADS * HG_DK == ROW_TILE * LANES and seq % BLOCK == 0
    lp = seq + BLOCK
    rp = batch * lp

    first = jnp.concatenate([jnp.zeros((PAD, d), x.dtype), meta_tokens.astype(x.dtype)], axis=0)
    h_pad = jnp.concatenate([jnp.broadcast_to(first[None], (batch, BLOCK, d)), x], axis=1).reshape(rp, d)

    lb = jax.nn.softmax(lb_logits.astype(F32), axis=0)[0]
    lv = diff_lambda[0].astype(F32)
    lam = (jnp.exp(jnp.sum(lv[0] * lv[1])) - jnp.exp(jnp.sum(lv[2] * lv[3])) + LAMBDA_INIT).reshape(1, 1)
    n_grp = d // ATT_HD
    gains = jnp.concatenate([jnp.tile(q_norm[0].astype(F32) * ATT_HD ** -0.5, n_grp),
                             jnp.tile(k_norm[0].astype(F32), n_grp),
                             jnp.ones(((N_SEG - 2) * d,), F32)])
    lbs = jnp.concatenate([jnp.zeros((4 * d,), F32), lb, jnp.zeros((4 * d,), F32)])
    colp = jnp.stack([gains, lbs])

    main, g = _in_proj(h_pad, norm1.astype(F32), w_in[0].astype(BF16), colp)

    blk = _largest_divisor(lp, (640, 512, 384, 256, 128))
    bias_d, bias_p = _bias_tiles(rel_bias, blk)
    subln = (diff_subln[0].astype(F32) * (1.0 - LAMBDA_INIT)).reshape(1, ATT_VD)
    o_att = _diff_attention(main, lam, bias_d, bias_p, subln, batch, lp, blk)

    o_hg = _hgrn2(main, g, hgrn_norm.astype(F32).reshape(1, HG_DK), batch, lp)

    rw_pad = jnp.pad(router_w[0].astype(F32), ((0, 0), (0, LANES - N_EXPERTS)))
    rb_pad = jnp.pad(router_b[0].astype(F32), (0, LANES - N_EXPERTS), constant_values=NEG_INF).reshape(1, LANES)
    h1, u2, topi, gate = _out_proj(main, o_att, o_hg, h_pad, w_out[0].astype(BF16), norm2.astype(F32),
                                   rw_pad, rb_pad)

    rank, counts = _moe_rank(topi, lp)

    rb = 512
    n_assign = batch * (seq + N_META) * TOP_K
    nblk = n_assign // rb + N_EXPERTS
    cnt = counts[0, :N_EXPERTS].astype(jnp.int32)
    padded = (cnt + rb - 1) // rb * rb
    pend = jnp.cumsum(padded)
    pstart = pend - padded
    rows = jnp.arange(rp, dtype=jnp.int32)
    valid = (rows % lp) >= PAD
    ti4 = topi[:, :TOP_K]
    dest = pstart[ti4] + rank[:, :TOP_K]
    dest_s = jnp.where(valid[:, None], dest, nblk * rb)
    slot_tok = jnp.zeros((nblk * rb,), jnp.int32).at[dest_s.reshape(-1)].set(
        jnp.repeat(rows, TOP_K), mode="drop")
    nact = (pend[-1] // rb).astype(jnp.int32)
    blk_ids = jnp.minimum(jnp.arange(nblk, dtype=jnp.int32), nact - 1)
    block_exp = jnp.minimum(jnp.searchsorted(pend, blk_ids * rb, side="right"), N_EXPERTS - 1).astype(jnp.int32)

    u2_rows = u2.reshape(rp * ROW_TILE, d // ROW_TILE)
    ys_rows = _moe_ffn(block_exp, nact.reshape(1), slot_tok.reshape(nblk, 1, rb), u2_rows,
                       w_gate_up[0], b_gate_up[0], w_down[0], b_down[0], rb)

    dest_x = dest.reshape(batch, lp, TOP_K)[:, BLOCK:, :].reshape(batch * seq // BLOCK, 1, BLOCK * TOP_K)
    return _combine(dest_x, ys_rows, gate, h1, batch, seq, lp)
```

```python
import functools
import math

import jax
import jax.numpy as jnp
from jax import lax
from jax.experimental import pallas as pl
from jax.experimental.pallas import tpu as pltpu

N_META = 16
BLOCK = 128
PAD = BLOCK - N_META

ATT_HEADS = 8
ATT_HD = 128
ATT_VD = 2 * ATT_HD

HG_HEADS = 16
HG_DK = 128
HG_CHUNK = 64
HG_SUB = 16

REL_BUCKETS = 32
REL_MAX_DIST = 128

N_EXPERTS = 32
TOP_K = 4
SWIGLU_LIMIT = 7.0
SWIGLU_ALPHA = 1.702

RMS_EPS = 1e-6
NEG_INF = -1e30
LAMBDA_INIT = 0.8 - 0.6 * math.exp(-0.3 * 0)
LOG2E = 1.0 / math.log(2.0)

N_SEG = 9
LANES = 128
SUBLANES = 8
ROW_TILE = 16
VMEM_LIMIT = 56 * 1024 * 1024

F32 = jnp.float32
BF16 = jnp.bfloat16


def _largest_divisor(n, candidates):
    for c in candidates:
        if n % c == 0:
            return c
    raise ValueError(f"no tile in {candidates} divides {n}")


def _sigmoid(x):
    return 1.0 / (1.0 + jnp.exp(-x))


def _inproj_kernel(x_ref, n1_ref, w_ref, cp_ref, main_ref, g_ref, u_ref, *, nsub, tn):
    j = pl.program_id(1)

    @pl.when(j == 0)
    def _():
        xf = x_ref[...]
        ms = jnp.mean(xf * xf, axis=-1, keepdims=True)
        u_ref[...] = (xf * lax.rsqrt(ms + RMS_EPS) * n1_ref[...]).astype(BF16)

    acc = jnp.dot(u_ref[...], w_ref[...], preferred_element_type=F32)
    seg = j // nsub

    @pl.when(seg <= 1)
    def _():
        for gi in range(tn // LANES):
            sl = slice(gi * LANES, (gi + 1) * LANES)
            y = acc[:, sl]
            ms = jnp.mean(y * y, axis=-1, keepdims=True)
            main_ref[0, :, sl] = (y * lax.rsqrt(ms + RMS_EPS) * cp_ref[0:1, sl]).astype(BF16)

    @pl.when((seg == 2) | (seg == 5))
    def _():
        main_ref[0] = acc.astype(BF16)

    @pl.when((seg == 3) | (seg == 6))
    def _():
        main_ref[0] = (acc * _sigmoid(acc)).astype(BF16)

    @pl.when(seg == 4)
    def _():
        lb = cp_ref[1:2, :]
        sg = _sigmoid(acc)
        g_ref[...] = jnp.log(lb + (1.0 - lb) * sg) * LOG2E
        main_ref[0] = ((1.0 - lb) * (1.0 - sg)).astype(BF16)

    @pl.when(seg >= 7)
    def _():
        main_ref[0] = _sigmoid(acc).astype(BF16)


def _in_proj(h_pad, n1, w_in_bf, colp):
    rp, d = h_pad.shape
    tm = _largest_divisor(rp, (512, 256, 128))
    tn = 1024
    nsub = d // tn
    nj = N_SEG * nsub
    kern = functools.partial(_inproj_kernel, nsub=nsub, tn=tn)
    return pl.pallas_call(
        kern,
        grid=(rp // tm, nj),
        in_specs=[
            pl.BlockSpec((tm, d), lambda i, j: (i, 0)),
            pl.BlockSpec((1, d), lambda i, j: (0, 0)),
            pl.BlockSpec((d, tn), lambda i, j: (0, j)),
            pl.BlockSpec((2, tn), lambda i, j: (0, j)),
        ],
        out_specs=[
            pl.BlockSpec((1, tm, tn), lambda i, j: (j // nsub, i, j % nsub)),
            pl.BlockSpec((tm, tn), lambda i, j: (i, jnp.clip(j - 4 * nsub, 0, nsub - 1))),
        ],
        out_shape=[
            jax.ShapeDtypeStruct((N_SEG, rp, d), BF16),
            jax.ShapeDtypeStruct((rp, d), F32),
        ],
        scratch_shapes=[pltpu.VMEM((tm, d), BF16)],
        compiler_params=pltpu.CompilerParams(
            dimension_semantics=("arbitrary", "arbitrary"), vmem_limit_bytes=VMEM_LIMIT),
        name="in_proj",
    )(h_pad, n1, w_in_bf, colp)


def _t5_bucket(rel):
    n = jnp.maximum(rel, 0)
    max_exact = REL_BUCKETS // 2
    nf = jnp.maximum(n, 1).astype(F32)
    large = max_exact + (jnp.log(nf / max_exact) / math.log(REL_MAX_DIST / max_exact)
                         * (REL_BUCKETS - max_exact)).astype(jnp.int32)
    large = jnp.minimum(large, REL_BUCKETS - 1)
    return jnp.where(n < max_exact, n, large)


def _bias_tiles(rel_bias, t):
    nh = rel_bias.shape[1]
    rb = (rel_bias.astype(F32) - rel_bias.astype(F32)[REL_BUCKETS - 1][None, :]) * LOG2E
    by_dist = rb[_t5_bucket(jnp.arange(2 * BLOCK, dtype=jnp.int32))].T
    r = jnp.arange(BLOCK, dtype=jnp.int32)[:, None]
    c = jnp.arange(BLOCK, dtype=jnp.int32)[None, :]
    d0 = jnp.where((r >= c)[None], by_dist[:, jnp.maximum(r - c, 0)], NEG_INF)
    d1 = by_dist[:, BLOCK + r - c]
    nb = t // BLOCK
    eye = jnp.eye(nb, dtype=F32)[None, :, None, :, None]
    sub = jnp.eye(nb, k=-1, dtype=F32)[None, :, None, :, None]
    upper = jnp.triu(jnp.ones((nb, nb), F32), 1)[None, :, None, :, None]
    corner = jnp.zeros((nb, nb), F32).at[0, nb - 1].set(1.0)[None, :, None, :, None]
    d0e = d0[:, None, :, None, :]
    d1e = d1[:, None, :, None, :]
    bd = jnp.where(upper > 0, NEG_INF, jnp.where(eye > 0, d0e, jnp.where(sub > 0, d1e, 0.0)))
    bp = jnp.where(corner > 0, d1e, 0.0)
    return bd.reshape(nh, t, t), bp.reshape(nh, t, t)


def _attn_kernel(qi_tab, ki_tab, lam_ref, q_ref, k_ref, v_ref, bd_ref, bp_ref, sg_ref, o_ref,
                 m_sc, l_sc, acc_sc, *, bq, bk):
    p = pl.program_id(2)
    qi = qi_tab[p]
    ki = ki_tab[p]

    @pl.when(ki == 0)
    def _():
        m_sc[...] = jnp.full(m_sc.shape, NEG_INF, F32)
        l_sc[...] = jnp.zeros(l_sc.shape, F32)
        acc_sc[...] = jnp.zeros(acc_sc.shape, F32)

    def scores(m):
        sl = slice(m * ATT_HD, (m + 1) * ATT_HD)
        return lax.dot_general(q_ref[0, :, sl], k_ref[0, :, sl], (((1,), (1,)), ((), ())),
                               preferred_element_type=F32)

    def update(m, s):
        m_old = m_sc[m]
        m_new = jnp.maximum(m_old, jnp.max(s, axis=-1, keepdims=True))
        alpha = jnp.exp2(m_old - m_new)
        pm = jnp.exp2(s - m_new)
        l_sc[m] = alpha * l_sc[m] + jnp.sum(pm, axis=-1, keepdims=True)
        acc_sc[m] = alpha * acc_sc[m] + jnp.dot(pm.astype(BF16), v_ref[0], preferred_element_type=F32)
        m_sc[m] = m_new

    def key_valid():
        col = ki * bk + lax.broadcasted_iota(jnp.int32, (1, bk), 1)
        return col >= PAD

    @pl.when(ki == qi)
    def _():
        ok = key_valid()
        for m in range(2):
            update(m, jnp.where(ok, scores(m) + bd_ref[0], NEG_INF))
        lam = lam_ref[0, 0]
        o = acc_sc[0] / l_sc[0] - lam * (acc_sc[1] / l_sc[1])
        ms = jnp.mean(o * o, axis=-1, keepdims=True)
        o_ref[...] = (o * lax.rsqrt(ms + RMS_EPS) * sg_ref[...]).astype(BF16)

    @pl.when(ki == qi - 1)
    def _():
        ok = key_valid()
        for m in range(2):
            update(m, jnp.where(ok, scores(m) + bp_ref[0], NEG_INF))

    @pl.when((ki < qi - 1) & (ki == 0))
    def _():
        ok = key_valid()
        for m in range(2):
            update(m, jnp.where(ok, scores(m), NEG_INF))

    @pl.when((ki < qi - 1) & (ki > 0))
    def _():
        for m in range(2):
            update(m, scores(m))


def _diff_attention(main, lam, bias_d, bias_p, subln, batch, lp, blk):
    _, rp, d = main.shape
    nb = lp // blk
    pairs = [(qi, ki) for qi in range(nb) for ki in range(qi + 1)]
    qi_tab = jnp.asarray([a for a, _ in pairs], jnp.int32)
    ki_tab = jnp.asarray([b for _, b in pairs], jnp.int32)
    kern = functools.partial(_attn_kernel, bq=blk, bk=blk)
    grid_spec = pltpu.PrefetchScalarGridSpec(
        num_scalar_prefetch=2,
        grid=(batch, ATT_HEADS, len(pairs)),
        in_specs=[
            pl.BlockSpec(memory_space=pltpu.SMEM),
            pl.BlockSpec((1, blk, ATT_VD), lambda b, h, p, qt, kt: (0, b * nb + qt[p], h)),
            pl.BlockSpec((1, blk, ATT_VD), lambda b, h, p, qt, kt: (1, b * nb + kt[p], h)),
            pl.BlockSpec((1, blk, ATT_VD), lambda b, h, p, qt, kt: (2, b * nb + kt[p], h)),
            pl.BlockSpec((1, blk, blk), lambda b, h, p, qt, kt: (h, 0, 0)),
            pl.BlockSpec((1, blk, blk), lambda b, h, p, qt, kt: (h, 0, 0)),
            pl.BlockSpec((1, ATT_VD), lambda b, h, p, qt, kt: (0, 0)),
        ],
        out_specs=pl.BlockSpec((blk, ATT_VD), lambda b, h, p, qt, kt: (b * nb + qt[p], h)),
        scratch_shapes=[
            pltpu.VMEM((2, blk, 1), F32),
            pltpu.VMEM((2, blk, 1), F32),
            pltpu.VMEM((2, blk, ATT_VD), F32),
        ],
    )
    return pl.pallas_call(
        kern,
        grid_spec=grid_spec,
        out_shape=jax.ShapeDtypeStruct((rp, d), BF16),
        compiler_params=pltpu.CompilerParams(
            dimension_semantics=("arbitrary", "arbitrary", "arbitrary"), vmem_limit_bytes=VMEM_LIMIT),
        name="diff_attn",
    )(qi_tab, ki_tab, lam, main, main, main, bias_d, bias_p, subln)


def _hgrn_kernel(q_ref, k_ref, v_ref, g_ref, gn_ref, o_ref, st_sc, b_sc, kf_sc, vf_sc, *, tc, hb):
    c_len = HG_CHUNK
    n_sub = HG_CHUNK // HG_SUB
    half = SUBLANES

    @pl.when(pl.program_id(2) == 0)
    def _():
        st_sc[...] = jnp.zeros(st_sc.shape, F32)

    ri = lax.broadcasted_iota(jnp.int32, (c_len, c_len), 0)
    ci = lax.broadcasted_iota(jnp.int32, (c_len, c_len), 1)
    tri = (ri >= ci).astype(F32)
    sub_r = ri // HG_SUB
    sub_c = ci // HG_SUB
    r2 = lax.broadcasted_iota(jnp.int32, (2 * LANES, 2 * LANES), 0) // LANES
    c2 = lax.broadcasted_iota(jnp.int32, (2 * LANES, 2 * LANES), 1) // LANES
    ones_bd = (r2 == c2).astype(BF16)
    row8 = lax.broadcasted_iota(jnp.int32, (half, LANES), 0)
    nt = (((1,), (1,)), ((), ()))

    def head(hh, r0, b_all):
        hs = slice(hh * HG_DK, (hh + 1) * HG_DK)
        q = q_ref[0, pl.ds(r0, c_len), hs].astype(F32)
        k = kf_sc[:, hs]
        v_bf = v_ref[0, pl.ds(r0, c_len), hs]
        b = b_all[:, hs]
        b_end = b[c_len - 1:c_len, :]
        st = st_sc[hh]

        o = lax.dot_general((q * jnp.exp2(b)).astype(BF16), st.astype(BF16), nt, preferred_element_type=F32)

        e_rows = [b[(j + 1) * HG_SUB - 1:(j + 1) * HG_SUB, :] for j in range(n_sub)]
        e_full = jnp.concatenate([jnp.broadcast_to(e, (HG_SUB, LANES)) for e in e_rows], axis=0)
        k_rel = (k * jnp.exp2(e_full - b)).astype(BF16)
        q_rel = jnp.concatenate(
            [(q * jnp.exp2(jnp.minimum(b - e_rows[j], 0.0))).astype(BF16) for j in range(n_sub - 1)], axis=0)
        a_all = lax.dot_general(q_rel, k_rel, nt, preferred_element_type=F32)
        a_off = jnp.zeros((c_len, c_len), F32)
        for j in range(n_sub - 1):
            a_off = jnp.where(sub_c == j, a_all[j * c_len:(j + 1) * c_len], a_off)
        a_off = jnp.where(sub_r > sub_c, a_off, 0.0)
        o = o + jnp.dot(a_off.astype(BF16), v_bf, preferred_element_type=F32)

        pieces = []
        meta = []
        for i in range(n_sub):
            for s in range(HG_SUB):
                row = i * HG_SUB + s
                ks = kf_sc[pl.ds(row, 1), hs]
                bs = b_sc[pl.ds(row, 1), hs]
                for hf in range(HG_SUB // half):
                    t0 = hf * half
                    if t0 + half - 1 < s:
                        continue
                    rows = slice(i * HG_SUB + t0, i * HG_SUB + t0 + half)
                    w = q[rows] * ks * jnp.exp2(jnp.minimum(b[rows] - bs, 0.0))
                    if s > t0:
                        w = jnp.where(row8 + t0 >= s, w, 0.0)
                    pieces.append(w.astype(BF16))
                    meta.append((i, s, hf))
        n_pairs = len(pieces) // 2
        lhs = jnp.concatenate(
            [jnp.concatenate([pieces[2 * n], pieces[2 * n + 1]], axis=1) for n in range(n_pairs)], axis=0)
        sums = jnp.dot(lhs, ones_bd, preferred_element_type=F32)
        diag = [[jnp.zeros((half, LANES), F32) for _ in range(HG_SUB // half)] for _ in range(n_sub)]
        for n, (i, s, hf) in enumerate(meta):
            blk = sums[(n // 2) * half:(n // 2 + 1) * half, (n % 2) * LANES:(n % 2 + 1) * LANES]
            vs = vf_sc[pl.ds(i * HG_SUB + s, 1), hs]
            diag[i][hf] = diag[i][hf] + blk * vs
        o = o + jnp.concatenate([d for row in diag for d in row], axis=0)

        k_out = (k * jnp.exp2(b_end - b)).astype(BF16)
        st_sc[hh] = st * jnp.exp2(b_end) + lax.dot_general(
            v_bf, k_out, (((0,), (0,)), ((), ())), preferred_element_type=F32)

        ms = jnp.mean(o * o, axis=-1, keepdims=True)
        o_ref[pl.ds(r0, c_len), hs] = (o * lax.rsqrt(ms + RMS_EPS) * gn_ref[...]).astype(BF16)

    def chunk(c, carry):
        r0 = pl.multiple_of(c * c_len, c_len)
        b_all = jnp.dot(tri, g_ref[pl.ds(r0, c_len), :], precision=lax.Precision.HIGHEST,
                        preferred_element_type=F32)
        b_sc[...] = b_all
        kf_sc[...] = k_ref[0, pl.ds(r0, c_len), :].astype(F32)
        vf_sc[...] = v_ref[0, pl.ds(r0, c_len), :].astype(F32)
        for hh in range(hb):
            head(hh, r0, b_all)
        return carry

    lax.fori_loop(0, tc // c_len, chunk, 0)


def _hgrn2(main, g, gn, batch, lp):
    _, rp, d = main.shape
    tc = _largest_divisor(lp, (640, 512, 256, 128, 64))
    nt_ = lp // tc
    hb = 4
    wb = hb * HG_DK
    kern = functools.partial(_hgrn_kernel, tc=tc, hb=hb)

    def seg_spec(seg):
        return pl.BlockSpec((1, tc, wb), lambda b, h, t: (seg, b * nt_ + t, h))

    return pl.pallas_call(
        kern,
        grid=(batch, HG_HEADS // hb, nt_),
        in_specs=[
            seg_spec(3), seg_spec(4), seg_spec(5),
            pl.BlockSpec((tc, wb), lambda b, h, t: (b * nt_ + t, h)),
            pl.BlockSpec((1, HG_DK), lambda b, h, t: (0, 0)),
        ],
        out_specs=pl.BlockSpec((tc, wb), lambda b, h, t: (b * nt_ + t, h)),
        out_shape=jax.ShapeDtypeStruct((rp, d), BF16),
        scratch_shapes=[
            pltpu.VMEM((hb, HG_DK, HG_DK), F32),
            pltpu.VMEM((HG_CHUNK, wb), F32),
            pltpu.VMEM((HG_CHUNK, wb), F32),
            pltpu.VMEM((HG_CHUNK, wb), F32),
        ],
        compiler_params=pltpu.CompilerParams(
            dimension_semantics=("arbitrary", "arbitrary", "arbitrary"), vmem_limit_bytes=VMEM_LIMIT),
        name="hgrn2",
    )(main, main, main, g, gn)


def _outproj_kernel(ga_ref, gh_ref, og_ref, oa_ref, oh_ref, h_ref, w_ref, n2_ref, rw_ref, rb_ref,
                    h1_ref, u2_ref, ti_ref, gt_ref):
    tm = h_ref.shape[0]
    y = (ga_ref[0].astype(F32) * oa_ref[...].astype(F32)
         + gh_ref[0].astype(F32) * (oh_ref[...].astype(F32) * og_ref[0].astype(F32)))
    h1 = h_ref[...] + jnp.dot(y.astype(BF16), w_ref[...], preferred_element_type=F32)
    h1_ref[...] = h1
    ms = jnp.mean(h1 * h1, axis=-1, keepdims=True)
    u2 = h1 * lax.rsqrt(ms + RMS_EPS) * n2_ref[...]
    for s in range(ROW_TILE):
        u2_ref[pl.ds(s, tm, stride=ROW_TILE), :] = u2[:, s * LANES:(s + 1) * LANES]
    logits = jnp.dot(u2, rw_ref[...], precision=lax.Precision.HIGHEST,
                     preferred_element_type=F32) + rb_ref[...]
    lane = lax.broadcasted_iota(jnp.int32, logits.shape, 1)
    cur = logits
    vals, idxs = [], []
    for _ in range(TOP_K):
        mx = jnp.max(cur, axis=-1, keepdims=True)
        ix = jnp.min(jnp.where(cur == mx, lane, LANES), axis=-1, keepdims=True)
        vals.append(mx)
        idxs.append(ix)
        cur = jnp.where(lane == ix, -jnp.inf, cur)
    es = [jnp.exp(v - vals[0]) for v in vals]
    inv = 1.0 / (es[0] + es[1] + es[2] + es[3])
    ti = jnp.zeros(logits.shape, jnp.int32)
    gt = jnp.zeros(logits.shape, F32)
    for kk in range(TOP_K):
        ti = jnp.where(lane == kk, idxs[kk], ti)
        gt = jnp.where(lane == kk, es[kk] * inv, gt)
    ti_ref[...] = ti
    gt_ref[...] = gt


def _out_proj(main, o_att, o_hg, h_pad, w_out_bf, n2, rw_pad, rb_pad):
    rp, d = h_pad.shape
    tm = _largest_divisor(rp, (256, 128))

    def seg_spec(seg):
        return pl.BlockSpec((1, tm, d), lambda i: (seg, i, 0))

    row = pl.BlockSpec((tm, d), lambda i: (i, 0))
    small = pl.BlockSpec((tm, LANES), lambda i: (i, 0))
    return pl.pallas_call(
        _outproj_kernel,
        grid=(rp // tm,),
        in_specs=[
            seg_spec(7), seg_spec(8), seg_spec(6), row, row, row,
            pl.BlockSpec((d, d), lambda i: (0, 0)),
            pl.BlockSpec((1, d), lambda i: (0, 0)),
            pl.BlockSpec((d, LANES), lambda i: (0, 0)),
            pl.BlockSpec((1, LANES), lambda i: (0, 0)),
        ],
        out_specs=[row, pl.BlockSpec((tm * ROW_TILE, LANES), lambda i: (i, 0)), small, small],
        out_shape=[
            jax.ShapeDtypeStruct((rp, d), F32),
            jax.ShapeDtypeStruct((rp * ROW_TILE, LANES), F32),
            jax.ShapeDtypeStruct((rp, LANES), jnp.int32),
            jax.ShapeDtypeStruct((rp, LANES), F32),
        ],
        compiler_params=pltpu.CompilerParams(
            dimension_semantics=("arbitrary",), vmem_limit_bytes=VMEM_LIMIT),
        name="out_proj",
    )(main, main, main, o_att, o_hg, h_pad, w_out_bf, n2, rw_pad, rb_pad)


def _rank_kernel(ti_ref, rank_ref, cnt_ref, carry_sc, *, tm, lp):
    i = pl.program_id(0)

    @pl.when(i == 0)
    def _():
        carry_sc[...] = jnp.zeros(carry_sc.shape, F32)

    ti = ti_ref[...]
    lane = lax.broadcasted_iota(jnp.int32, (tm, LANES), 1)
    row = i * tm + lax.broadcasted_iota(jnp.int32, (tm, 1), 0)
    valid = lax.rem(row, lp) >= PAD
    hot = [(lane == ti[:, kk:kk + 1]) & valid for kk in range(TOP_K)]
    any_hot = hot[0] | hot[1] | hot[2] | hot[3]
    any_f = jnp.where(any_hot, 1.0, 0.0)
    rr = lax.broadcasted_iota(jnp.int32, (tm, tm), 0)
    cc = lax.broadcasted_iota(jnp.int32, (tm, tm), 1)
    strict = jnp.where(rr > cc, 1.0, 0.0).astype(BF16)
    base = carry_sc[...] + jnp.dot(strict, any_f.astype(BF16), preferred_element_type=F32)
    rank = jnp.zeros((tm, LANES), F32)
    for kk in range(TOP_K):
        rk = jnp.sum(jnp.where(hot[kk], base, 0.0), axis=-1, keepdims=True)
        rank = jnp.where(lane == kk, rk, rank)
    rank_ref[...] = rank.astype(jnp.int32)
    carry_sc[...] = carry_sc[...] + jnp.sum(any_f, axis=0, keepdims=True)
    cnt_ref[...] = carry_sc[...]


def _moe_rank(topi, lp):
    rp = topi.shape[0]
    tm = _largest_divisor(rp, (256, 128))
    kern = functools.partial(_rank_kernel, tm=tm, lp=lp)
    return pl.pallas_call(
        kern,
        grid=(rp // tm,),
        in_specs=[pl.BlockSpec((tm, LANES), lambda i: (i, 0))],
        out_specs=[pl.BlockSpec((tm, LANES), lambda i: (i, 0)), pl.BlockSpec((1, LANES), lambda i: (0, 0))],
        out_shape=[jax.ShapeDtypeStruct((rp, LANES), jnp.int32), jax.ShapeDtypeStruct((1, LANES), F32)],
        scratch_shapes=[pltpu.VMEM((1, LANES), F32)],
        compiler_params=pltpu.CompilerParams(dimension_semantics=("arbitrary",)),
        name="moe_rank",
    )(topi)


def _row_gather_start(src_hbm, dst, sem, tok, r):
    pltpu.make_async_copy(
        src_hbm.at[pl.ds(pl.multiple_of(tok * ROW_TILE, ROW_TILE), ROW_TILE), :],
        dst.at[pl.ds(pl.multiple_of(r * ROW_TILE, ROW_TILE), ROW_TILE), :],
        sem).start()


def _ffn_kernel(bexp_ref, nact_ref, tokc_ref, tokn_ref, u2_hbm, wgu_ref, bgu_ref, wd_ref, bd_ref, ys_ref,
                gbuf, xb, acc, sem, *, rb, nff, tff):
    i = pl.program_id(0)
    f = pl.program_id(1)
    nact = nact_ref[0]
    slot = lax.rem(i, 2)

    def issue(tok_ref, sl):
        def body(r, carry):
            _row_gather_start(u2_hbm, gbuf.at[sl], sem.at[sl], tok_ref[0, 0, r], r)
            return carry
        lax.fori_loop(0, rb, body, 0, unroll=8)

    @pl.when((f == 0) & (i == 0))
    def _():
        issue(tokc_ref, 0)

    @pl.when((f == 0) & (i + 1 < nact))
    def _():
        issue(tokn_ref, 1 - slot)

    @pl.when((f == 0) & (i < nact))
    def _():
        pltpu.make_async_copy(gbuf.at[slot], gbuf.at[slot], sem.at[slot]).wait()
        for s in range(ROW_TILE):
            xb[:, s * LANES:(s + 1) * LANES] = gbuf[slot, pl.ds(s, rb, stride=ROW_TILE), :].astype(BF16)
        acc[...] = jnp.broadcast_to(bd_ref[0], acc.shape)

    @pl.when(i < nact)
    def _():
        gu = jnp.dot(xb[...], wgu_ref[0, 0], preferred_element_type=F32) + bgu_ref[0, 0]
        xg = jnp.minimum(gu[:, :tff], SWIGLU_LIMIT)
        xl = jnp.clip(gu[:, tff:], -SWIGLU_LIMIT, SWIGLU_LIMIT)
        act = xg * _sigmoid(SWIGLU_ALPHA * xg) * (xl + 1.0)
        acc[...] += jnp.dot(act.astype(BF16), wd_ref[0], preferred_element_type=F32)

    @pl.when((f == nff - 1) & (i < nact))
    def _():
        for s in range(ROW_TILE):
            ys_ref[pl.ds(s, rb, stride=ROW_TILE), :] = acc[:, s * LANES:(s + 1) * LANES]

    @pl.when((f == nff - 1) & (i >= nact))
    def _():
        ys_ref[...] = jnp.zeros(ys_ref.shape, F32)


def _moe_ffn(block_exp, nact, slot_tok, u2_rows, wgu, bgu, wdn, bdn, rb):
    n_e, nff, d, tff2 = wgu.shape
    tff = tff2 // 2
    nblk = slot_tok.shape[0]
    kern = functools.partial(_ffn_kernel, rb=rb, nff=nff, tff=tff)

    def f_eff(i, f, na):
        return jnp.where(i < na[0], f, nff - 1)

    grid_spec = pltpu.PrefetchScalarGridSpec(
        num_scalar_prefetch=2,
        grid=(nblk, nff),
        in_specs=[
            pl.BlockSpec((1, 1, rb), lambda i, f, be, na: (i, 0, 0), memory_space=pltpu.SMEM),
            pl.BlockSpec((1, 1, rb), lambda i, f, be, na: (jnp.minimum(i + 1, nblk - 1), 0, 0),
                         memory_space=pltpu.SMEM),
            pl.BlockSpec(memory_space=pl.ANY),
            pl.BlockSpec((1, 1, d, tff2), lambda i, f, be, na: (be[i], f_eff(i, f, na), 0, 0)),
            pl.BlockSpec((1, 1, 1, tff2), lambda i, f, be, na: (be[i], f_eff(i, f, na), 0, 0)),
            pl.BlockSpec((1, tff, d), lambda i, f, be, na: (be[i], f_eff(i, f, na), 0)),
            pl.BlockSpec((1, 1, d), lambda i, f, be, na: (be[i], 0, 0)),
        ],
        out_specs=pl.BlockSpec((rb * ROW_TILE, LANES), lambda i, f, be, na: (i, 0)),
        scratch_shapes=[
            pltpu.VMEM((2, rb * ROW_TILE, LANES), F32),
            pltpu.VMEM((rb, d), BF16),
            pltpu.VMEM((rb, d), F32),
            pltpu.SemaphoreType.DMA((2,)),
        ],
    )
    return pl.pallas_call(
        kern,
        grid_spec=grid_spec,
        out_shape=jax.ShapeDtypeStruct((nblk * rb * ROW_TILE, LANES), F32),
        compiler_params=pltpu.CompilerParams(
            dimension_semantics=("arbitrary", "arbitrary"), vmem_limit_bytes=VMEM_LIMIT),
        name="moe_ffn",
    )(block_exp, nact, slot_tok, slot_tok, u2_rows, wgu, bgu, wdn, bdn)


def _combine_kernel(dc_ref, dn_ref, ys_hbm, gt_ref, h1_ref, o_ref, cbuf, sem, *, tm, n_steps, n_i):
    n = pl.program_id(0) * n_i + pl.program_id(1)
    slot = lax.rem(n, 2)

    def issue(dref, sl):
        def body(r, carry):
            for kk in range(TOP_K):
                _row_gather_start(ys_hbm, cbuf.at[sl, kk], sem.at[sl], dref[0, 0, r * TOP_K + kk], r)
            return carry
        lax.fori_loop(0, tm, body, 0, unroll=4)

    @pl.when(n == 0)
    def _():
        issue(dc_ref, 0)

    @pl.when(n + 1 < n_steps)
    def _():
        issue(dn_ref, 1 - slot)

    pltpu.make_async_copy(cbuf.at[slot], cbuf.at[slot], sem.at[slot]).wait()
    gates = [jnp.broadcast_to(gt_ref[:, kk:kk + 1], (tm, LANES)) for kk in range(TOP_K)]
    for s in range(ROW_TILE):
        sl = slice(s * LANES, (s + 1) * LANES)
        y = h1_ref[:, sl]
        for kk in range(TOP_K):
            y = y + gates[kk] * cbuf[slot, kk, pl.ds(s, tm, stride=ROW_TILE), :]
        o_ref[0, :, sl] = y


def _combine(dest_x, ys_rows, gate, h1, batch, seq, lp):
    rp, d = h1.shape
    tm = BLOCK
    n_i = seq // tm
    n_steps = batch * n_i
    nlp = lp // tm
    kern = functools.partial(_combine_kernel, tm=tm, n_steps=n_steps, n_i=n_i)
    return pl.pallas_call(
        kern,
        grid=(batch, n_i),
        in_specs=[
            pl.BlockSpec((1, 1, tm * TOP_K), lambda b, i: (b * n_i + i, 0, 0), memory_space=pltpu.SMEM),
            pl.BlockSpec((1, 1, tm * TOP_K), lambda b, i: (jnp.minimum(b * n_i + i + 1, n_steps - 1), 0, 0),
                         memory_space=pltpu.SMEM),
            pl.BlockSpec(memory_space=pl.ANY),
            pl.BlockSpec((tm, LANES), lambda b, i: (b * nlp + 1 + i, 0)),
            pl.BlockSpec((tm, d), lambda b, i: (b * nlp + 1 + i, 0)),
        ],
        out_specs=pl.BlockSpec((1, tm, d), lambda b, i: (b, i, 0)),
        out_shape=jax.ShapeDtypeStruct((batch, seq, d), F32),
        scratch_shapes=[
            pltpu.VMEM((2, TOP_K, tm * ROW_TILE, LANES), F32),
            pltpu.SemaphoreType.DMA((2,)),
        ],
        compiler_params=pltpu.CompilerParams(
            dimension_semantics=("arbitrary", "arbitrary"), vmem_limit_bytes=VMEM_LIMIT),
        name="moe_combine",
    )(dest_x, dest_x, ys_rows, gate, h1)


def kernel(x, meta_tokens, rel_bias, lb_logits, norm1, w_in, q_norm, k_norm, diff_lambda, diff_subln,
           hgrn_norm, w_out, norm2, router_w, router_b, w_gate_up, b_gate_up, w_down, b_down):
    batch, seq, d = x.shape
    assert norm1.shape[0] == 1, "single-layer block"
    assert d == ATT_HEADS * ATT_VD == HG_HEADS * HG_DK == ROW_TILE * LANES and seq % BLOCK == 0
    lp = seq + BLOCK
    rp = batch * lp

    first = jnp.concatenate([jnp.zeros((PAD, d), x.dtype), meta_tokens.astype(x.dtype)], axis=0)
    h_pad = jnp.concatenate([jnp.broadcast_to(first[None], (batch, BLOCK, d)), x], axis=1).reshape(rp, d)

    lb = jax.nn.softmax(lb_logits.astype(F32), axis=0)[0]
    lv = diff_lambda[0].astype(F32)
    lam = (jnp.exp(jnp.sum(lv[0] * lv[1])) - jnp.exp(jnp.sum(lv[2] * lv[3])) + LAMBDA_INIT).reshape(1, 1)
    n_grp = d // ATT_HD
    gains = jnp.concatenate([jnp.tile(q_norm[0].astype(F32) * (ATT_HD ** -0.5 * LOG2E), n_grp),
                             jnp.tile(k_norm[0].astype(F32), n_grp),
                             jnp.ones(((N_SEG - 2) * d,), F32)])
    lbs = jnp.concatenate([jnp.zeros((4 * d,), F32), lb, jnp.zeros((4 * d,), F32)])
    colp = jnp.stack([gains, lbs])

    main, g = _in_proj(h_pad, norm1.astype(F32), w_in[0].astype(BF16), colp)

    blk = _largest_divisor(lp, (640, 512, 384, 256, 128))
    bias_d, bias_p = _bias_tiles(rel_bias, blk)
    subln = (diff_subln[0].astype(F32) * (1.0 - LAMBDA_INIT)).reshape(1, ATT_VD)
    o_att = _diff_attention(main, lam, bias_d, bias_p, subln, batch, lp, blk)

    o_hg = _hgrn2(main, g, hgrn_norm.astype(F32).reshape(1, HG_DK), batch, lp)

    rw_pad = jnp.pad(router_w[0].astype(F32), ((0, 0), (0, LANES - N_EXPERTS)))
    rb_pad = jnp.pad(router_b[0].astype(F32), (0, LANES - N_EXPERTS), constant_values=NEG_INF).reshape(1, LANES)
    h1, u2_rows, topi, gate = _out_proj(main, o_att, o_hg, h_pad, w_out[0].astype(BF16), norm2.astype(F32),
                                        rw_pad, rb_pad)

    rank, counts = _moe_rank(topi, lp)

    rb = 512
    n_assign = batch * (seq + N_META) * TOP_K
    nblk = n_assign // rb + N_EXPERTS
    cnt = counts[0, :N_EXPERTS].astype(jnp.int32)
    padded = (cnt + rb - 1) // rb * rb
    pend = jnp.cumsum(padded)
    pstart = pend - padded
    rows = jnp.arange(rp, dtype=jnp.int32)
    valid = (rows % lp) >= PAD
    ti4 = topi[:, :TOP_K]
    dest = pstart[ti4] + rank[:, :TOP_K]
    dest_s = jnp.where(valid[:, None], dest, nblk * rb)
    slot_tok = jnp.zeros((nblk * rb,), jnp.int32).at[dest_s.reshape(-1)].set(
        jnp.repeat(rows, TOP_K), mode="drop")
    nact = (pend[-1] // rb).astype(jnp.int32)
    blk_ids = jnp.minimum(jnp.arange(nblk, dtype=jnp.int32), nact - 1)
    block_exp = jnp.minimum(jnp.searchsorted(pend, blk_ids * rb, side="right"), N_EXPERTS - 1).astype(jnp.int32)

    n_e, _, f2 = w_gate_up.shape[1:]
    dff = f2 // 2
    tff = 512
    nff = dff // tff
    wgu = w_gate_up[0].astype(BF16).reshape(n_e, d, 2, nff, tff).transpose(0, 3, 1, 2, 4).reshape(
        n_e, nff, d, 2 * tff)
    bgu = b_gate_up[0].astype(F32).reshape(n_e, 2, nff, tff).transpose(0, 2, 1, 3).reshape(n_e, nff, 1, 2 * tff)
    ys_rows = _moe_ffn(block_exp, nact.reshape(1), slot_tok.reshape(nblk, 1, rb), u2_rows,
                       wgu, bgu, w_down[0].astype(BF16), b_down[0].astype(F32).reshape(n_e, 1, d), rb)

    dest_x = dest.reshape(batch, lp, TOP_K)[:, BLOCK:, :].reshape(batch * seq // BLOCK, 1, BLOCK * TOP_K)
    return _combine(dest_x, ys_rows, gate, h1, batch, seq, lp)
```

```python
import functools
import math

import jax
import jax.numpy as jnp
from jax import lax
from jax.experimental import pallas as pl
from jax.experimental.pallas import tpu as pltpu

N_META = 16
BLOCK = 128
PAD = BLOCK - N_META

ATT_HEADS = 8
ATT_HD = 128
ATT_VD = 2 * ATT_HD

HG_HEADS = 16
HG_DK = 128
HG_CHUNK = 64
HG_SUB = 16

REL_BUCKETS = 32
REL_MAX_DIST = 128

N_EXPERTS = 32
TOP_K = 4
SWIGLU_LIMIT = 7.0
SWIGLU_ALPHA = 1.702

RMS_EPS = 1e-6
NEG_INF = -1e30
LAMBDA_INIT = 0.8 - 0.6 * math.exp(-0.3 * 0)
LOG2E = 1.0 / math.log(2.0)

N_SEG = 9
LANES = 128
SUBLANES = 8
ROW_TILE = 16
VMEM_LIMIT = 56 * 1024 * 1024

F32 = jnp.float32
BF16 = jnp.bfloat16


def _largest_divisor(n, candidates):
    for c in candidates:
        if n % c == 0:
            return c
    raise ValueError(f"no tile in {candidates} divides {n}")


def _sigmoid(x):
    return 1.0 / (1.0 + jnp.exp(-x))


def _inproj_kernel(x_ref, n1_ref, w_ref, cp_ref, main_ref, g_ref, u_ref, *, nsub, tn):
    j = pl.program_id(1)

    @pl.when(j == 0)
    def _():
        xf = x_ref[...]
        ms = jnp.mean(xf * xf, axis=-1, keepdims=True)
        u_ref[...] = (xf * lax.rsqrt(ms + RMS_EPS) * n1_ref[...]).astype(BF16)

    acc = jnp.dot(u_ref[...], w_ref[...], preferred_element_type=F32)
    seg = j // nsub

    @pl.when(seg <= 1)
    def _():
        for gi in range(tn // LANES):
            sl = slice(gi * LANES, (gi + 1) * LANES)
            y = acc[:, sl]
            ms = jnp.mean(y * y, axis=-1, keepdims=True)
            main_ref[0, :, sl] = (y * lax.rsqrt(ms + RMS_EPS) * cp_ref[0:1, sl]).astype(BF16)

    @pl.when((seg == 2) | (seg == 5))
    def _():
        main_ref[0] = acc.astype(BF16)

    @pl.when((seg == 3) | (seg == 6))
    def _():
        main_ref[0] = (acc * _sigmoid(acc)).astype(BF16)

    @pl.when(seg == 4)
    def _():
        lb = cp_ref[1:2, :]
        sg = _sigmoid(acc)
        g_ref[...] = jnp.log(lb + (1.0 - lb) * sg) * LOG2E
        main_ref[0] = ((1.0 - lb) * (1.0 - sg)).astype(BF16)

    @pl.when(seg >= 7)
    def _():
        main_ref[0] = _sigmoid(acc).astype(BF16)


def _in_proj(h_pad, n1, w_in_bf, colp):
    rp, d = h_pad.shape
    tm = _largest_divisor(rp, (512, 256, 128))
    tn = 1024
    nsub = d // tn
    nj = N_SEG * nsub
    kern = functools.partial(_inproj_kernel, nsub=nsub, tn=tn)
    return pl.pallas_call(
        kern,
        grid=(rp // tm, nj),
        in_specs=[
            pl.BlockSpec((tm, d), lambda i, j: (i, 0)),
            pl.BlockSpec((1, d), lambda i, j: (0, 0)),
            pl.BlockSpec((d, tn), lambda i, j: (0, j)),
            pl.BlockSpec((2, tn), lambda i, j: (0, j)),
        ],
        out_specs=[
            pl.BlockSpec((1, tm, tn), lambda i, j: (j // nsub, i, j % nsub)),
            pl.BlockSpec((tm, tn), lambda i, j: (i, jnp.clip(j - 4 * nsub, 0, nsub - 1))),
        ],
        out_shape=[
            jax.ShapeDtypeStruct((N_SEG, rp, d), BF16),
            jax.ShapeDtypeStruct((rp, d), F32),
        ],
        scratch_shapes=[pltpu.VMEM((tm, d), BF16)],
        compiler_params=pltpu.CompilerParams(
            dimension_semantics=("arbitrary", "arbitrary"), vmem_limit_bytes=VMEM_LIMIT),
        name="in_proj",
    )(h_pad, n1, w_in_bf, colp)


def _t5_bucket(rel):
    n = jnp.maximum(rel, 0)
    max_exact = REL_BUCKETS // 2
    nf = jnp.maximum(n, 1).astype(F32)
    large = max_exact + (jnp.log(nf / max_exact) / math.log(REL_MAX_DIST / max_exact)
                         * (REL_BUCKETS - max_exact)).astype(jnp.int32)
    large = jnp.minimum(large, REL_BUCKETS - 1)
    return jnp.where(n < max_exact, n, large)


def _bias_patterns(rel_bias):
    rb =(rel_bias.astype(F32) - rel_bias.astype(F32)[REL_BUCKETS - 1][None, :]) * LOG2E
    by_dist = rb[_t5_bucket(jnp.arange(2 * BLOCK, dtype=jnp.int32))].T
    r = jnp.arange(BLOCK, dtype=jnp.int32)[:, None]
    c = jnp.arange(BLOCK, dtype=jnp.int32)[None, :]
    d0 = jnp.where((r >= c)[None], by_dist[:, jnp.maximum(r - c, 0)], NEG_INF)
    d1 = by_dist[:, BLOCK + r - c]
    return d0, d1


def _tile_bias(d0, d1, nb, diagonal):
    zero = jnp.zeros((BLOCK, BLOCK), F32)
    neg = jnp.full((BLOCK, BLOCK), NEG_INF, F32)
    rows = []
    for a in range(nb):
        if diagonal:
            blocks = [d0 if c == a else d1 if c == a - 1 else neg if c > a else zero for c in range(nb)]
        else:
            blocks = [d1 if (a == 0 and c == nb - 1) else zero for c in range(nb)]
        rows.append(jnp.concatenate(blocks, axis=1))
    return jnp.concatenate(rows, axis=0)


def _attn_kernel(qi_tab, ki_tab, lam_ref, q_ref, k_ref, v_ref, d0_ref, d1_ref, sg_ref, o_ref,
                 m_sc, l_sc, acc_sc, *, bq, bk):
    p = pl.program_id(2)
    qi = qi_tab[p]
    ki = ki_tab[p]
    nlb = bk // LANES

    @pl.when(ki == 0)
    def _():
        m_sc[...] = jnp.full(m_sc.shape, NEG_INF, BF16).astype(F32)
        l_sc[...] = jnp.zeros(l_sc.shape, F32)
        acc_sc[...] = jnp.zeros(acc_sc.shape, F32)

    def scores(m):
        sl = slice(m * ATT_HD, (m + 1) * ATT_HD)
        return lax.dot_general(q_ref[0, :, sl], k_ref[0, :, sl], (((1,), (1,)), ((), ())),
                               preferred_element_type=F32)

    def update(m, s):
        sb = s.astype(BF16)
        lane_blocks = [sb[:, c * LANES:(c + 1) * LANES] for c in range(nlb)]
        m_old = m_sc[m]
        blk_max = functools.reduce(jnp.maximum, lane_blocks).astype(F32)
        m_new = jnp.maximum(m_old, jnp.max(blk_max, axis=-1, keepdims=True))
        alpha = jnp.exp2(m_old - m_new)
        pb = jnp.exp2(sb - m_new.astype(BF16))
        part = functools.reduce(jnp.add, [pb[:, c * LANES:(c + 1) * LANES] for c in range(nlb)]).astype(F32)
        l_sc[m] = alpha * l_sc[m] + jnp.sum(part, axis=-1, keepdims=True)
        acc_sc[m] = alpha * acc_sc[m] + jnp.dot(pb, v_ref[0], preferred_element_type=F32)
        m_sc[m] = m_new

    def key_valid():
        col = ki * bk + lax.broadcasted_iota(jnp.int32, (1, bk), 1)
        return col >= PAD

    @pl.when(ki == qi)
    def _():
        ok = key_valid()
        bias = _tile_bias(d0_ref[0], d1_ref[0], bk // BLOCK, True)
        for m in range(2):
            update(m, jnp.where(ok, scores(m) + bias, NEG_INF))
        lam = lam_ref[0, 0]
        o = acc_sc[0] / l_sc[0] - lam * (acc_sc[1] / l_sc[1])
        ms = jnp.mean(o * o, axis=-1, keepdims=True)
        o_ref[...] = (o * lax.rsqrt(ms + RMS_EPS) * sg_ref[...]).astype(BF16)

    @pl.when(ki == qi - 1)
    def _():
        ok = key_valid()
        bias = _tile_bias(d0_ref[0], d1_ref[0], bk // BLOCK, False)
        for m in range(2):
            update(m, jnp.where(ok, scores(m) + bias, NEG_INF))

    @pl.when((ki < qi - 1) & (ki == 0))
    def _():
        ok = key_valid()
        for m in range(2):
            update(m, jnp.where(ok, scores(m), NEG_INF))

    @pl.when((ki < qi - 1) & (ki > 0))
    def _():
        for m in range(2):
            update(m, scores(m))


def _diff_attention(main, lam, pat_d0, pat_d1, subln, batch, lp, blk):
    _, rp, d = main.shape
    nb = lp // blk
    pairs = [(qi, ki) for qi in range(nb) for ki in range(qi + 1)]
    qi_tab = jnp.asarray([a for a, _ in pairs], jnp.int32)
    ki_tab = jnp.asarray([b for _, b in pairs], jnp.int32)
    kern = functools.partial(_attn_kernel, bq=blk, bk=blk)
    grid_spec = pltpu.PrefetchScalarGridSpec(
        num_scalar_prefetch=2,
        grid=(batch, ATT_HEADS, len(pairs)),
        in_specs=[
            pl.BlockSpec(memory_space=pltpu.SMEM),
            pl.BlockSpec((1, blk, ATT_VD), lambda b, h, p, qt, kt: (0, b * nb + qt[p], h)),
            pl.BlockSpec((1, blk, ATT_VD), lambda b, h, p, qt, kt: (1, b * nb + kt[p], h)),
            pl.BlockSpec((1, blk, ATT_VD), lambda b, h, p, qt, kt: (2, b * nb + kt[p], h)),
            pl.BlockSpec((1, BLOCK, BLOCK), lambda b, h, p, qt, kt: (h, 0, 0)),
            pl.BlockSpec((1, BLOCK, BLOCK), lambda b, h, p, qt, kt: (h, 0, 0)),
            pl.BlockSpec((1, ATT_VD), lambda b, h, p, qt, kt: (0, 0)),
        ],
        out_specs=pl.BlockSpec((blk, ATT_VD), lambda b, h, p, qt, kt: (b * nb + qt[p], h)),
        scratch_shapes=[
            pltpu.VMEM((2, blk, 1), F32),
            pltpu.VMEM((2, blk, 1), F32),
            pltpu.VMEM((2, blk, ATT_VD), F32),
        ],
    )
    return pl.pallas_call(
        kern,
        grid_spec=grid_spec,
        out_shape=jax.ShapeDtypeStruct((rp, d), BF16),
        compiler_params=pltpu.CompilerParams(
            dimension_semantics=("arbitrary", "arbitrary", "arbitrary"), vmem_limit_bytes=VMEM_LIMIT),
        name="diff_attn",
    )(qi_tab, ki_tab, lam, main, main, main, pat_d0, pat_d1, subln)


def _hgrn_kernel(q_ref, k_ref, v_ref, g_ref, gn_ref, o_ref, st_sc, b_sc, kf_sc, vf_sc, *, tc, hb):
    c_len = HG_CHUNK
    n_sub = HG_CHUNK // HG_SUB
    half = SUBLANES

    @pl.when(pl.program_id(2) == 0)
    def _():
        st_sc[...] = jnp.zeros(st_sc.shape, F32)

    ri = lax.broadcasted_iota(jnp.int32, (c_len, c_len), 0)
    ci = lax.broadcasted_iota(jnp.int32, (c_len, c_len), 1)
    tri = (ri >= ci).astype(F32)
    sub_r = ri // HG_SUB
    sub_c = ci // HG_SUB
    r2 = lax.broadcasted_iota(jnp.int32, (2 * LANES, 2 * LANES), 0) // LANES
    c2 = lax.broadcasted_iota(jnp.int32, (2 * LANES, 2 * LANES), 1) // LANES
    ones_bd = (r2 == c2).astype(BF16)
    row8 = lax.broadcasted_iota(jnp.int32, (half, LANES), 0)
    nt = (((1,), (1,)), ((), ()))

    def head(hh, r0, b_all):
        hs = slice(hh * HG_DK, (hh + 1) * HG_DK)
        q = q_ref[0, pl.ds(r0, c_len), hs].astype(F32)
        k = kf_sc[:, hs]
        v_bf = v_ref[0, pl.ds(r0, c_len), hs]
        b = b_all[:, hs]
        b_end = b[c_len - 1:c_len, :]
        st = st_sc[hh]

        o = lax.dot_general((q * jnp.exp2(b)).astype(BF16), st.astype(BF16), nt, preferred_element_type=F32)

        e_rows = [b[(j + 1) * HG_SUB - 1:(j + 1) * HG_SUB, :] for j in range(n_sub)]
        e_full = jnp.concatenate([jnp.broadcast_to(e, (HG_SUB, LANES)) for e in e_rows], axis=0)
        k_rel = (k * jnp.exp2(e_full - b)).astype(BF16)
        q_rel = jnp.concatenate(
            [(q * jnp.exp2(jnp.minimum(b - e_rows[j], 0.0))).astype(BF16) for j in range(n_sub - 1)], axis=0)
        a_all = lax.dot_general(q_rel, k_rel, nt, preferred_element_type=F32)
        a_off = jnp.zeros((c_len, c_len), F32)
        for j in range(n_sub - 1):
            a_off = jnp.where(sub_c == j, a_all[j * c_len:(j + 1) * c_len], a_off)
        a_off = jnp.where(sub_r > sub_c, a_off, 0.0)
        o = o + jnp.dot(a_off.astype(BF16), v_bf, preferred_element_type=F32)

        pieces = []
        meta = []
        for i in range(n_sub):
            for s in range(HG_SUB):
                row = i * HG_SUB + s
                ks = kf_sc[pl.ds(row, 1), hs]
                bs = b_sc[pl.ds(row, 1), hs]
                for hf in range(HG_SUB // half):
                    t0 = hf * half
                    if t0 + half - 1 < s:
                        continue
                    rows = slice(i * HG_SUB + t0, i * HG_SUB + t0 + half)
                    w = q[rows] * ks * jnp.exp2(jnp.minimum(b[rows] - bs, 0.0))
                    if s > t0:
                        w = jnp.where(row8 + t0 >= s, w, 0.0)
                    pieces.append(w.astype(BF16))
                    meta.append((i, s, hf))
        n_pairs = len(pieces) // 2
        lhs = jnp.concatenate(
            [jnp.concatenate([pieces[2 * n], pieces[2 * n + 1]], axis=1) for n in range(n_pairs)], axis=0)
        sums = jnp.dot(lhs, ones_bd, preferred_element_type=F32)
        diag = [[jnp.zeros((half, LANES), F32) for _ in range(HG_SUB // half)] for _ in range(n_sub)]
        for n, (i, s, hf) in enumerate(meta):
            blk = sums[(n // 2) * half:(n // 2 + 1) * half, (n % 2) * LANES:(n % 2 + 1) * LANES]
            vs = vf_sc[pl.ds(i * HG_SUB + s, 1), hs]
            diag[i][hf] = diag[i][hf] + blk * vs
        o = o + jnp.concatenate([d for row in diag for d in row], axis=0)

        k_out = (k * jnp.exp2(b_end - b)).astype(BF16)
        st_sc[hh] = st * jnp.exp2(b_end) + lax.dot_general(
            v_bf, k_out, (((0,), (0,)), ((), ())), preferred_element_type=F32)

        ms = jnp.mean(o * o, axis=-1, keepdims=True)
        o_ref[pl.ds(r0, c_len), hs] = (o * lax.rsqrt(ms + RMS_EPS) * gn_ref[...]).astype(BF16)

    def chunk(c, carry):
        r0 = pl.multiple_of(c * c_len, c_len)
        b_all = jnp.dot(tri, g_ref[pl.ds(r0, c_len), :], precision=lax.Precision.HIGHEST,
                        preferred_element_type=F32)
        b_sc[...] = b_all
        kf_sc[...] = k_ref[0, pl.ds(r0, c_len), :].astype(F32)
        vf_sc[...] = v_ref[0, pl.ds(r0, c_len), :].astype(F32)
        for hh in range(hb):
            head(hh, r0, b_all)
        return carry

    lax.fori_loop(0, tc // c_len, chunk, 0)


def _hgrn2(main, g, gn, batch, lp):
    _, rp, d = main.shape
    tc = _largest_divisor(lp, (640, 512, 256, 128, 64))
    nt_ = lp // tc
    hb = 4
    wb = hb * HG_DK
    kern = functools.partial(_hgrn_kernel, tc=tc, hb=hb)

    def seg_spec(seg):
        return pl.BlockSpec((1, tc, wb), lambda b, h, t: (seg, b * nt_ + t, h))

    return pl.pallas_call(
        kern,
        grid=(batch, HG_HEADS // hb, nt_),
        in_specs=[
            seg_spec(3), seg_spec(4), seg_spec(5),
            pl.BlockSpec((tc, wb), lambda b, h, t: (b * nt_ + t, h)),
            pl.BlockSpec((1, HG_DK), lambda b, h, t: (0, 0)),
        ],
        out_specs=pl.BlockSpec((tc, wb), lambda b, h, t: (b * nt_ + t, h)),
        out_shape=jax.ShapeDtypeStruct((rp, d), BF16),
        scratch_shapes=[
            pltpu.VMEM((hb, HG_DK, HG_DK), F32),
            pltpu.VMEM((HG_CHUNK, wb), F32),
            pltpu.VMEM((HG_CHUNK, wb), F32),
            pltpu.VMEM((HG_CHUNK, wb), F32),
        ],
        compiler_params=pltpu.CompilerParams(
            dimension_semantics=("arbitrary", "arbitrary", "arbitrary"), vmem_limit_bytes=VMEM_LIMIT),
        name="hgrn2",
    )(main, main, main, g, gn)


def _outproj_kernel(ga_ref, gh_ref, og_ref, oa_ref, oh_ref, h_ref, w_ref, n2_ref, rw_ref, rb_ref,
                    h1_ref, u2_ref, ti_ref, gt_ref):
    tm = h_ref.shape[0]
    y = (ga_ref[0].astype(F32) * oa_ref[...].astype(F32)
         + gh_ref[0].astype(F32) * (oh_ref[...].astype(F32) * og_ref[0].astype(F32)))
    h1 = h_ref[...] + jnp.dot(y.astype(BF16), w_ref[...], preferred_element_type=F32)
    h1_ref[...] = h1
    ms = jnp.mean(h1 * h1, axis=-1, keepdims=True)
    u2 = h1 * lax.rsqrt(ms + RMS_EPS) * n2_ref[...]
    for s in range(ROW_TILE):
        u2_ref[pl.ds(s, tm, stride=ROW_TILE), :] = u2[:, s * LANES:(s + 1) * LANES]
    logits = jnp.dot(u2, rw_ref[...], precision=lax.Precision.HIGHEST,
                     preferred_element_type=F32) + rb_ref[...]
    lane = lax.broadcasted_iota(jnp.int32, logits.shape, 1)
    cur = logits
    vals, idxs = [], []
    for _ in range(TOP_K):
        mx = jnp.max(cur, axis=-1, keepdims=True)
        ix = jnp.min(jnp.where(cur == mx, lane, LANES), axis=-1, keepdims=True)
        vals.append(mx)
        idxs.append(ix)
        cur = jnp.where(lane == ix, -jnp.inf, cur)
    es = [jnp.exp(v - vals[0]) for v in vals]
    inv = 1.0 / (es[0] + es[1] + es[2] + es[3])
    ti = jnp.zeros(logits.shape, jnp.int32)
    gt = jnp.zeros(logits.shape, F32)
    for kk in range(TOP_K):
        ti = jnp.where(lane == kk, idxs[kk], ti)
        gt = jnp.where(lane == kk, es[kk] * inv, gt)
    ti_ref[...] = ti
    gt_ref[...] = gt


def _out_proj(main, o_att, o_hg, h_pad, w_out_bf, n2, rw_pad, rb_pad):
    rp, d = h_pad.shape
    tm = _largest_divisor(rp, (256, 128))

    def seg_spec(seg):
        return pl.BlockSpec((1, tm, d), lambda i: (seg, i, 0))

    row = pl.BlockSpec((tm, d), lambda i: (i, 0))
    small = pl.BlockSpec((tm, LANES), lambda i: (i, 0))
    return pl.pallas_call(
        _outproj_kernel,
        grid=(rp // tm,),
        in_specs=[
            seg_spec(7), seg_spec(8), seg_spec(6), row, row, row,
            pl.BlockSpec((d, d), lambda i: (0, 0)),
            pl.BlockSpec((1, d), lambda i: (0, 0)),
            pl.BlockSpec((d, LANES), lambda i: (0, 0)),
            pl.BlockSpec((1, LANES), lambda i: (0, 0)),
        ],
        out_specs=[row, pl.BlockSpec((tm * ROW_TILE, LANES), lambda i: (i, 0)), small, small],
        out_shape=[
            jax.ShapeDtypeStruct((rp, d), F32),
            jax.ShapeDtypeStruct((rp * ROW_TILE, LANES), F32),
            jax.ShapeDtypeStruct((rp, LANES), jnp.int32),
            jax.ShapeDtypeStruct((rp, LANES), F32),
        ],
        compiler_params=pltpu.CompilerParams(
            dimension_semantics=("arbitrary",), vmem_limit_bytes=VMEM_LIMIT),
        name="out_proj",
    )(main, main, main, o_att, o_hg, h_pad, w_out_bf, n2, rw_pad, rb_pad)


def _rank_kernel(ti_ref, rank_ref, cnt_ref, carry_sc, *, tm, lp):
    i = pl.program_id(0)

    @pl.when(i == 0)
    def _():
        carry_sc[...] = jnp.zeros(carry_sc.shape, F32)

    ti = ti_ref[...]
    lane = lax.broadcasted_iota(jnp.int32, (tm, LANES), 1)
    row = i * tm + lax.broadcasted_iota(jnp.int32, (tm, 1), 0)
    valid = lax.rem(row, lp) >= PAD
    hot = [(lane == ti[:, kk:kk + 1]) & valid for kk in range(TOP_K)]
    any_hot = hot[0] | hot[1] | hot[2] | hot[3]
    any_f = jnp.where(any_hot, 1.0, 0.0)
    rr = lax.broadcasted_iota(jnp.int32, (tm, tm), 0)
    cc = lax.broadcasted_iota(jnp.int32, (tm, tm), 1)
    strict = jnp.where(rr > cc, 1.0, 0.0).astype(BF16)
    base = carry_sc[...] + jnp.dot(strict, any_f.astype(BF16), preferred_element_type=F32)
    rank = jnp.zeros((tm, LANES), F32)
    for kk in range(TOP_K):
        rk = jnp.sum(jnp.where(hot[kk], base, 0.0), axis=-1, keepdims=True)
        rank = jnp.where(lane == kk, rk, rank)
    rank_ref[...] = rank.astype(jnp.int32)
    carry_sc[...] = carry_sc[...] + jnp.sum(any_f, axis=0, keepdims=True)
    cnt_ref[...] = carry_sc[...]


def _moe_rank(topi, lp):
    rp = topi.shape[0]
    tm = _largest_divisor(rp, (256, 128))
    kern = functools.partial(_rank_kernel, tm=tm, lp=lp)
    return pl.pallas_call(
        kern,
        grid=(rp // tm,),
        in_specs=[pl.BlockSpec((tm, LANES), lambda i: (i, 0))],
        out_specs=[pl.BlockSpec((tm, LANES), lambda i: (i, 0)), pl.BlockSpec((1, LANES), lambda i: (0, 0))],
        out_shape=[jax.ShapeDtypeStruct((rp, LANES), jnp.int32), jax.ShapeDtypeStruct((1, LANES), F32)],
        scratch_shapes=[pltpu.VMEM((1, LANES), F32)],
        compiler_params=pltpu.CompilerParams(dimension_semantics=("arbitrary",)),
        name="moe_rank",
    )(topi)


def _row_gather_start(src_hbm, dst, sem, tok, r):
    pltpu.make_async_copy(
        src_hbm.at[pl.ds(pl.multiple_of(tok * ROW_TILE, ROW_TILE), ROW_TILE), :],
        dst.at[pl.ds(pl.multiple_of(r * ROW_TILE, ROW_TILE), ROW_TILE), :],
        sem).start()


def _ffn_kernel(bexp_ref, nact_ref, tokc_ref, tokn_ref, u2_hbm, wgu_hbm, wdn_hbm, bgu_ref, bdn_ref, ys_ref,
                gbuf, ring, gu_sc, acc_sc, gsem, wsem, *, rb, d, dff, tk):
    i = pl.program_id(0)
    nact = nact_ref[0]
    slot = lax.rem(i, 2)
    nka = d // tk
    nkb = dff // tk
    n_chunks = 2 * nka + nkb
    n_ring = 3
    assert n_chunks % n_ring == 0 and dff == d

    def chunk_copy(blk, c, ring_slot):
        e = bexp_ref[blk]
        if c < 2 * nka:
            kt, half = divmod(c, 2)
            src = wgu_hbm.at[e, pl.ds(kt * tk, tk), pl.ds(half * dff, dff)]
        else:
            src = wdn_hbm.at[e, pl.ds((c - 2 * nka) * tk, tk), :]
        return pltpu.make_async_copy(src, ring.at[ring_slot], wsem.at[ring_slot])

    def issue_rows(tok_ref, sl):
        def body(r, carry):
            _row_gather_start(u2_hbm, gbuf.at[sl], gsem.at[sl], tok_ref[0, 0, r], r)
            return carry
        lax.fori_loop(0, rb, body, 0, unroll=8)

    @pl.when(i == 0)
    def _():
        issue_rows(tokc_ref, 0)
        chunk_copy(0, 0, 0).start()
        chunk_copy(0, 1, 1).start()

    @pl.when(i + 1 < nact)
    def _():
        issue_rows(tokn_ref, 1 - slot)

    @pl.when(i < nact)
    def _():
        pltpu.make_async_copy(gbuf.at[slot], gbuf.at[slot], gsem.at[slot]).wait()
        gu_sc[...] = jnp.broadcast_to(bgu_ref[0], gu_sc.shape)
        acc_sc[...] = jnp.broadcast_to(bdn_ref[0], acc_sc.shape)
        xk = None
        act = None
        for c in range(n_chunks):
            rs = c % n_ring
            chunk_copy(i, c, rs).wait()
            nxt = c + n_ring - 1
            if nxt < n_chunks:
                chunk_copy(i, nxt, nxt % n_ring).start()
            else:
                chunk_copy(i + 1, nxt - n_chunks, nxt % n_ring).start()
            w = ring[rs].astype(BF16)
            if c < 2 * nka:
                kt, half = divmod(c, 2)
                if half == 0:
                    per = tk // LANES
                    xk = jnp.concatenate(
                        [gbuf[slot, pl.ds(kt * per + s, rb, stride=ROW_TILE), :].astype(BF16)
                         for s in range(per)], axis=1)
                cols = slice(half * dff, (half + 1) * dff)
                gu_sc[:, cols] += jnp.dot(xk, w, preferred_element_type=F32)
            else:
                kt = c - 2 * nka
                if kt == 0:
                    xg = jnp.minimum(gu_sc[:, :dff], SWIGLU_LIMIT)
                    xl = jnp.clip(gu_sc[:, dff:], -SWIGLU_LIMIT, SWIGLU_LIMIT)
                    act = (xg * _sigmoid(SWIGLU_ALPHA * xg) * (xl + 1.0)).astype(BF16)
                acc_sc[...] += jnp.dot(act[:, kt * tk:(kt + 1) * tk], w, preferred_element_type=F32)
        for s in range(ROW_TILE):
            ys_ref[pl.ds(s, rb, stride=ROW_TILE), :] = acc_sc[:, s * LANES:(s + 1) * LANES]

    @pl.when(i == nact)
    def _():
        chunk_copy(i, 0, 0).wait()
        chunk_copy(i, 1, 1).wait()

    @pl.when(i >= nact)
    def _():
        ys_ref[...] = jnp.zeros(ys_ref.shape, F32)


def _moe_ffn(block_exp, nact, slot_tok, u2_rows, w_gu, b_gu, w_dn, b_dn, rb):
    n_e, d, f2 = w_gu.shape
    dff = f2 // 2
    nblk = slot_tok.shape[0]
    tk = 512
    kern = functools.partial(_ffn_kernel, rb=rb, d=d, dff=dff, tk=tk)
    grid_spec = pltpu.PrefetchScalarGridSpec(
        num_scalar_prefetch=2,
        grid=(nblk,),
        in_specs=[
            pl.BlockSpec((1, 1, rb), lambda i, be, na: (i, 0, 0), memory_space=pltpu.SMEM),
            pl.BlockSpec((1, 1, rb), lambda i, be, na: (jnp.minimum(i + 1, nblk - 1), 0, 0),
                         memory_space=pltpu.SMEM),
            pl.BlockSpec(memory_space=pl.ANY),
            pl.BlockSpec(memory_space=pl.ANY),
            pl.BlockSpec(memory_space=pl.ANY),
            pl.BlockSpec((1, 1, f2), lambda i, be, na: (be[i], 0, 0)),
            pl.BlockSpec((1, 1, d), lambda i, be, na: (be[i], 0, 0)),
        ],
        out_specs=pl.BlockSpec((rb * ROW_TILE, LANES), lambda i, be, na: (i, 0)),
        scratch_shapes=[
            pltpu.VMEM((2, rb * ROW_TILE, LANES), F32),
            pltpu.VMEM((3, tk, d), F32),
            pltpu.VMEM((rb, f2), F32),
            pltpu.VMEM((rb, d), F32),
            pltpu.SemaphoreType.DMA((2,)),
            pltpu.SemaphoreType.DMA((3,)),
        ],
    )
    return pl.pallas_call(
        kern,
        grid_spec=grid_spec,
        out_shape=jax.ShapeDtypeStruct((nblk * rb * ROW_TILE, LANES), F32),
        compiler_params=pltpu.CompilerParams(
            dimension_semantics=("arbitrary",), vmem_limit_bytes=VMEM_LIMIT),
        name="moe_ffn",
    )(block_exp, nact, slot_tok, slot_tok, u2_rows, w_gu, w_dn, b_gu.reshape(n_e, 1, f2),
      b_dn.reshape(n_e, 1, d))


def _combine_kernel(dc_ref, dn_ref, ys_hbm, gt_ref, h1_ref, o_ref, cbuf, sem, *, tm, n_steps, n_i):
    n = pl.program_id(0) * n_i + pl.program_id(1)
    slot = lax.rem(n, 2)

    def issue(dref, sl):
        def body(r, carry):
            for kk in range(TOP_K):
                _row_gather_start(ys_hbm, cbuf.at[sl, kk], sem.at[sl], dref[0, 0, r * TOP_K + kk], r)
            return carry
        lax.fori_loop(0, tm, body, 0, unroll=4)

    @pl.when(n == 0)
    def _():
        issue(dc_ref, 0)

    @pl.when(n + 1 < n_steps)
    def _():
        issue(dn_ref, 1 - slot)

    pltpu.make_async_copy(cbuf.at[slot], cbuf.at[slot], sem.at[slot]).wait()
    gates = [jnp.broadcast_to(gt_ref[:, kk:kk + 1], (tm, LANES)) for kk in range(TOP_K)]
    for s in range(ROW_TILE):
        sl = slice(s * LANES, (s + 1) * LANES)
        y = h1_ref[:, sl]
        for kk in range(TOP_K):
            y = y + gates[kk] * cbuf[slot, kk, pl.ds(s, tm, stride=ROW_TILE), :]
        o_ref[0, :, sl] = y


def _combine(dest_x, ys_rows, gate, h1, batch, seq, lp):
    rp, d = h1.shape
    tm = BLOCK
    n_i = seq // tm
    n_steps = batch * n_i
    nlp = lp // tm
    kern = functools.partial(_combine_kernel, tm=tm, n_steps=n_steps, n_i=n_i)
    return pl.pallas_call(
        kern,
        grid=(batch, n_i),
        in_specs=[
            pl.BlockSpec((1, 1, tm * TOP_K), lambda b, i: (b * n_i + i, 0, 0), memory_space=pltpu.SMEM),
            pl.BlockSpec((1, 1, tm * TOP_K), lambda b, i: (jnp.minimum(b * n_i + i + 1, n_steps - 1), 0, 0),
                         memory_space=pltpu.SMEM),
            pl.BlockSpec(memory_space=pl.ANY),
            pl.BlockSpec((tm, LANES), lambda b, i: (b * nlp + 1 + i, 0)),
            pl.BlockSpec((tm, d), lambda b, i: (b * nlp + 1 + i, 0)),
        ],
        out_specs=pl.BlockSpec((1, tm, d), lambda b, i: (b, i, 0)),
        out_shape=jax.ShapeDtypeStruct((batch, seq, d), F32),
        scratch_shapes=[
            pltpu.VMEM((2, TOP_K, tm * ROW_TILE, LANES), F32),
            pltpu.SemaphoreType.DMA((2,)),
        ],
        compiler_params=pltpu.CompilerParams(
            dimension_semantics=("arbitrary", "arbitrary"), vmem_limit_bytes=VMEM_LIMIT),
        name="moe_combine",
    )(dest_x, dest_x, ys_rows, gate, h1)


def kernel(x, meta_tokens, rel_bias, lb_logits, norm1, w_in, q_norm, k_norm, diff_lambda, diff_subln,
           hgrn_norm, w_out, norm2, router_w, router_b, w_gate_up, b_gate_up, w_down, b_down):
    batch, seq, d = x.shape
    assert norm1.shape[0] == 1, "single-layer block"
    assert d == ATT_HEADS * ATT_VD == HG_HEADS * HG_DK == ROW_TILE * LANES and seq % BLOCK == 0
    lp = seq + BLOCK
    rp = batch * lp

    first = jnp.concatenate([jnp.zeros((PAD, d), x.dtype), meta_tokens.astype(x.dtype)], axis=0)
    h_pad = jnp.concatenate([jnp.broadcast_to(first[None], (batch, BLOCK, d)), x], axis=1).reshape(rp, d)

    lb = jax.nn.softmax(lb_logits.astype(F32), axis=0)[0]
    lv = diff_lambda[0].astype(F32)
    lam = (jnp.exp(jnp.sum(lv[0] * lv[1])) - jnp.exp(jnp.sum(lv[2] * lv[3])) + LAMBDA_INIT).reshape(1, 1)
    n_grp = d // ATT_HD
    gains = jnp.concatenate([jnp.tile(q_norm[0].astype(F32) * (ATT_HD ** -0.5 * LOG2E), n_grp),
                             jnp.tile(k_norm[0].astype(F32), n_grp),
                             jnp.ones(((N_SEG - 2) * d,), F32)])
    lbs = jnp.concatenate([jnp.zeros((4 * d,), F32), lb, jnp.zeros((4 * d,), F32)])
    colp = jnp.stack([gains, lbs])

    main, g = _in_proj(h_pad, norm1.astype(F32), w_in[0].astype(BF16), colp)

    blk = _largest_divisor(lp, (640, 512, 384, 256, 128))
    pat_d0, pat_d1 = _bias_patterns(rel_bias)
    subln = (diff_subln[0].astype(F32) * (1.0 - LAMBDA_INIT)).reshape(1, ATT_VD)
    o_att = _diff_attention(main, lam, pat_d0, pat_d1, subln, batch, lp, blk)

    o_hg = _hgrn2(main, g, hgrn_norm.astype(F32).reshape(1, HG_DK), batch, lp)

    rw_pad = jnp.pad(router_w[0].astype(F32), ((0, 0), (0, LANES - N_EXPERTS)))
    rb_pad = jnp.pad(router_b[0].astype(F32), (0, LANES - N_EXPERTS), constant_values=NEG_INF).reshape(1, LANES)
    h1, u2_rows, topi, gate = _out_proj(main, o_att, o_hg, h_pad, w_out[0].astype(BF16), norm2.astype(F32),
                                        rw_pad, rb_pad)

    rank, counts = _moe_rank(topi, lp)

    rb = 512
    n_assign = batch * (seq + N_META) * TOP_K
    nblk = n_assign // rb + N_EXPERTS + 1
    cnt = counts[0, :N_EXPERTS].astype(jnp.int32)
    padded = (cnt + rb - 1) // rb * rb
    pend = jnp.cumsum(padded)
    pstart = pend - padded
    rows = jnp.arange(rp, dtype=jnp.int32)
    valid = (rows % lp) >= PAD
    ti4 = topi[:, :TOP_K]
    dest = pstart[ti4] + rank[:, :TOP_K]
    dest_s = jnp.where(valid[:, None], dest, nblk * rb)
    slot_tok = jnp.zeros((nblk * rb,), jnp.int32).at[dest_s.reshape(-1)].set(
        jnp.repeat(rows, TOP_K), mode="drop")
    nact = (pend[-1] // rb).astype(jnp.int32)
    blk_ids = jnp.minimum(jnp.arange(nblk, dtype=jnp.int32), nact - 1)
    block_exp = jnp.minimum(jnp.searchsorted(pend, blk_ids * rb, side="right"), N_EXPERTS - 1).astype(jnp.int32)

    ys_rows = _moe_ffn(block_exp, nact.reshape(1), slot_tok.reshape(nblk, 1, rb), u2_rows,
                       w_gate_up[0], b_gate_up[0].astype(F32), w_down[0], b_down[0].astype(F32), rb)

    dest_x = dest.reshape(batch, lp, TOP_K)[:, BLOCK:, :].reshape(batch * seq // BLOCK, 1, BLOCK * TOP_K)
    return _combine(dest_x, ys_rows, gate, h1, batch, seq, lp)
```

```python
import functools
import math

import jax
import jax.numpy as jnp
from jax import lax
from jax.experimental import pallas as pl
from jax.experimental.pallas import tpu as pltpu

N_META = 16
BLOCK = 128
PAD = BLOCK - N_META

ATT_HEADS = 8
ATT_HD = 128
ATT_VD = 2 * ATT_HD

HG_HEADS = 16
HG_DK = 128
HG_CHUNK = 64
HG_SUB = 16

REL_BUCKETS = 32
REL_MAX_DIST = 128

N_EXPERTS = 32
TOP_K = 4
SWIGLU_LIMIT = 7.0
SWIGLU_ALPHA = 1.702

RMS_EPS = 1e-6
NEG_INF = -1e30
LAMBDA_INIT = 0.8 - 0.6 * math.exp(-0.3 * 0)
LOG2E = 1.0 / math.log(2.0)

N_SEG = 9
LANES = 128
SUBLANES = 8
ROW_TILE = 16
VMEM_LIMIT = 56 * 1024 * 1024

F32 = jnp.float32
BF16 = jnp.bfloat16


def _largest_divisor(n, candidates):
    for c in candidates:
        if n % c == 0:
            return c
    raise ValueError(f"no tile in {candidates} divides {n}")


def _sigmoid(x):
    return 1.0 / (1.0 + jnp.exp(-x))


def _inproj_kernel(x_ref, n1_ref, w_ref, cp_ref, main_ref, g_ref, u_ref, *, nsub, tn):
    j = pl.program_id(1)

    @pl.when(j == 0)
    def _():
        xf = x_ref[...]
        ms = jnp.mean(xf * xf, axis=-1, keepdims=True)
        u_ref[...] = (xf * lax.rsqrt(ms + RMS_EPS) * n1_ref[...]).astype(BF16)

    acc = jnp.dot(u_ref[...], w_ref[...], preferred_element_type=F32)
    seg = j // nsub

    @pl.when(seg <= 1)
    def _():
        for gi in range(tn // LANES):
            sl = slice(gi * LANES, (gi + 1) * LANES)
            y = acc[:, sl]
            ms = jnp.mean(y * y, axis=-1, keepdims=True)
            main_ref[0, :, sl] = (y * lax.rsqrt(ms + RMS_EPS) * cp_ref[0:1, sl]).astype(BF16)

    @pl.when((seg == 2) | (seg == 5))
    def _():
        main_ref[0] = acc.astype(BF16)

    @pl.when((seg == 3) | (seg == 6))
    def _():
        main_ref[0] = (acc * _sigmoid(acc)).astype(BF16)

    @pl.when(seg == 4)
    def _():
        lb = cp_ref[1:2, :]
        sg = _sigmoid(acc)
        g_ref[...] = jnp.log(lb + (1.0 - lb) * sg) * LOG2E
        main_ref[0] = ((1.0 - lb) * (1.0 - sg)).astype(BF16)

    @pl.when(seg >= 7)
    def _():
        main_ref[0] = _sigmoid(acc).astype(BF16)


def _in_proj(h_pad, n1, w_in_bf, colp):
    rp, d = h_pad.shape
    tm = _largest_divisor(rp, (512, 256, 128))
    tn = 1024
    nsub = d // tn
    nj = N_SEG * nsub
    kern = functools.partial(_inproj_kernel, nsub=nsub, tn=tn)
    return pl.pallas_call(
        kern,
        grid=(rp // tm, nj),
        in_specs=[
            pl.BlockSpec((tm, d), lambda i, j: (i, 0)),
            pl.BlockSpec((1, d), lambda i, j: (0, 0)),
            pl.BlockSpec((d, tn), lambda i, j: (0, j)),
            pl.BlockSpec((2, tn), lambda i, j: (0, j)),
        ],
        out_specs=[
            pl.BlockSpec((1, tm, tn), lambda i, j: (j // nsub, i, j % nsub)),
            pl.BlockSpec((tm, tn), lambda i, j: (i, jnp.clip(j - 4 * nsub, 0, nsub - 1))),
        ],
        out_shape=[
            jax.ShapeDtypeStruct((N_SEG, rp, d), BF16),
            jax.ShapeDtypeStruct((rp, d), F32),
        ],
        scratch_shapes=[pltpu.VMEM((tm, d), BF16)],
        compiler_params=pltpu.CompilerParams(
            dimension_semantics=("arbitrary", "arbitrary"), vmem_limit_bytes=VMEM_LIMIT),
        name="in_proj",
    )(h_pad, n1, w_in_bf, colp)


def _t5_bucket(rel):
    n = jnp.maximum(rel, 0)
    max_exact = REL_BUCKETS // 2
    nf = jnp.maximum(n, 1).astype(F32)
    large = max_exact + (jnp.log(nf / max_exact) / math.log(REL_MAX_DIST / max_exact)
                         * (REL_BUCKETS - max_exact)).astype(jnp.int32)
    large = jnp.minimum(large, REL_BUCKETS - 1)
    return jnp.where(n < max_exact, n, large)


def _bias_patterns(rel_bias):
    rb =(rel_bias.astype(F32) - rel_bias.astype(F32)[REL_BUCKETS - 1][None, :]) * LOG2E
    by_dist = rb[_t5_bucket(jnp.arange(2 * BLOCK, dtype=jnp.int32))].T
    r = jnp.arange(BLOCK, dtype=jnp.int32)[:, None]
    c = jnp.arange(BLOCK, dtype=jnp.int32)[None, :]
    d0 = jnp.where((r >= c)[None], by_dist[:, jnp.maximum(r - c, 0)], NEG_INF)
    d1 = by_dist[:, BLOCK + r - c]
    return d0, d1


def _tile_bias(d0, d1, nb, diagonal):
    zero = jnp.zeros((BLOCK, BLOCK), F32)
    neg = jnp.full((BLOCK, BLOCK), NEG_INF, F32)
    rows = []
    for a in range(nb):
        if diagonal:
            blocks = [d0 if c == a else d1 if c == a - 1 else neg if c > a else zero for c in range(nb)]
        else:
            blocks = [d1 if (a == 0 and c == nb - 1) else zero for c in range(nb)]
        rows.append(jnp.concatenate(blocks, axis=1))
    return jnp.concatenate(rows, axis=0)


def _attn_kernel(qi_tab, ki_tab, lam_ref, q_ref, k_ref, v_ref, d0_ref, d1_ref, sg_ref, o_ref,
                 m_sc, l_sc, acc_sc, *, bq, bk, hpb):
    p = pl.program_id(2)
    qi = qi_tab[p]
    ki = ki_tab[p]
    nlb = bk // LANES

    @pl.when(ki == 0)
    def _():
        m_sc[...] = jnp.full(m_sc.shape, NEG_INF, BF16).astype(F32)
        l_sc[...] = jnp.zeros(l_sc.shape, F32)
        acc_sc[...] = jnp.zeros(acc_sc.shape, F32)

    def scores(hh, m):
        sl = slice(hh * ATT_VD + m * ATT_HD, hh * ATT_VD + (m + 1) * ATT_HD)
        return lax.dot_general(q_ref[0, :, sl], k_ref[0, :, sl], (((1,), (1,)), ((), ())),
                               preferred_element_type=F32)

    def update(hh, m, s):
        c_idx = 2 * hh + m
        sb = s.astype(BF16)
        lane_blocks = [sb[:, c * LANES:(c + 1) * LANES] for c in range(nlb)]
        m_old = m_sc[c_idx]
        blk_max = functools.reduce(jnp.maximum, lane_blocks).astype(F32)
        m_new = jnp.maximum(m_old, jnp.max(blk_max, axis=-1, keepdims=True))
        alpha = jnp.exp2(m_old - m_new)
        pb = jnp.exp2(sb - m_new.astype(BF16))
        part = functools.reduce(jnp.add, [pb[:, c * LANES:(c + 1) * LANES] for c in range(nlb)]).astype(F32)
        l_sc[c_idx] = alpha * l_sc[c_idx] + jnp.sum(part, axis=-1, keepdims=True)
        acc_sc[c_idx] = alpha * acc_sc[c_idx] + jnp.dot(
            pb, v_ref[0, :, hh * ATT_VD:(hh + 1) * ATT_VD], preferred_element_type=F32)
        m_sc[c_idx] = m_new

    def key_valid():
        col = ki * bk + lax.broadcasted_iota(jnp.int32, (1, bk), 1)
        return col >= PAD

    def sweep(bias_of, masked):
        ok = key_valid() if masked else None
        for hh in range(hpb):
            bias = bias_of(hh)
            for m in range(2):
                s = scores(hh, m)
                if bias is not None:
                    s = s + bias
                if masked:
                    s = jnp.where(ok, s, NEG_INF)
                update(hh, m, s)

    @pl.when(ki == qi)
    def _():
        sweep(lambda hh: _tile_bias(d0_ref[hh], d1_ref[hh], bk // BLOCK, True), True)
        lam = lam_ref[0, 0]
        for hh in range(hpb):
            o = acc_sc[2 * hh] / l_sc[2 * hh] - lam * (acc_sc[2 * hh + 1] / l_sc[2 * hh + 1])
            ms = jnp.mean(o * o, axis=-1, keepdims=True)
            o_ref[:, hh * ATT_VD:(hh + 1) * ATT_VD] = (o * lax.rsqrt(ms + RMS_EPS) * sg_ref[...]).astype(BF16)

    @pl.when(ki == qi - 1)
    def _():
        sweep(lambda hh: _tile_bias(d0_ref[hh], d1_ref[hh], bk // BLOCK, False), True)

    @pl.when((ki < qi - 1) & (ki == 0))
    def _():
        sweep(lambda hh: None, True)

    @pl.when((ki < qi - 1) & (ki > 0))
    def _():
        sweep(lambda hh: None, False)


def _diff_attention(main, lam, pat_d0, pat_d1, subln, batch, lp, blk):
    _, rp, d = main.shape
    nb = lp // blk
    hpb = 2
    wb = hpb * ATT_VD
    pairs = [(qi, ki) for qi in range(nb) for ki in range(qi + 1)]
    qi_tab = jnp.asarray([a for a, _ in pairs], jnp.int32)
    ki_tab = jnp.asarray([b for _, b in pairs], jnp.int32)
    kern = functools.partial(_attn_kernel, bq=blk, bk=blk, hpb=hpb)
    grid_spec = pltpu.PrefetchScalarGridSpec(
        num_scalar_prefetch=2,
        grid=(batch, ATT_HEADS // hpb, len(pairs)),
        in_specs=[
            pl.BlockSpec(memory_space=pltpu.SMEM),
            pl.BlockSpec((1, blk, wb), lambda b, h, p, qt, kt: (0, b * nb + qt[p], h)),
            pl.BlockSpec((1, blk, wb), lambda b, h, p, qt, kt: (1, b * nb + kt[p], h)),
            pl.BlockSpec((1, blk, wb), lambda b, h, p, qt, kt: (2, b * nb + kt[p], h)),
            pl.BlockSpec((hpb, BLOCK, BLOCK), lambda b, h, p, qt, kt: (h, 0, 0)),
            pl.BlockSpec((hpb, BLOCK, BLOCK), lambda b, h, p, qt, kt: (h, 0, 0)),
            pl.BlockSpec((1, ATT_VD), lambda b, h, p, qt, kt: (0, 0)),
        ],
        out_specs=pl.BlockSpec((blk, wb), lambda b, h, p, qt, kt: (b * nb + qt[p], h)),
        scratch_shapes=[
            pltpu.VMEM((2 * hpb, blk, 1), F32),
            pltpu.VMEM((2 * hpb, blk, 1), F32),
            pltpu.VMEM((2 * hpb, blk, ATT_VD), F32),
        ],
    )
    return pl.pallas_call(
        kern,
        grid_spec=grid_spec,
        out_shape=jax.ShapeDtypeStruct((rp, d), BF16),
        compiler_params=pltpu.CompilerParams(
            dimension_semantics=("arbitrary", "arbitrary", "arbitrary"), vmem_limit_bytes=VMEM_LIMIT),
        name="diff_attn",
    )(qi_tab, ki_tab, lam, main, main, main, pat_d0, pat_d1, subln)


def _hgrn_kernel(q_ref, k_ref, v_ref, g_ref, gn_ref, o_ref, st_sc, b_sc, kf_sc, vf_sc, *, tc, hb):
    c_len = HG_CHUNK
    n_sub = HG_CHUNK // HG_SUB
    half = SUBLANES

    @pl.when(pl.program_id(2) == 0)
    def _():
        st_sc[...] = jnp.zeros(st_sc.shape, F32)

    ri = lax.broadcasted_iota(jnp.int32, (c_len, c_len), 0)
    ci = lax.broadcasted_iota(jnp.int32, (c_len, c_len), 1)
    tri = (ri >= ci).astype(F32)
    sub_r = ri // HG_SUB
    sub_c = ci // HG_SUB
    r2 = lax.broadcasted_iota(jnp.int32, (2 * LANES, 2 * LANES), 0) // LANES
    c2 = lax.broadcasted_iota(jnp.int32, (2 * LANES, 2 * LANES), 1) // LANES
    ones_bd = (r2 == c2).astype(BF16)
    row8 = lax.broadcasted_iota(jnp.int32, (half, LANES), 0)
    nt = (((1,), (1,)), ((), ()))

    def head(hh, r0, b_all):
        hs = slice(hh * HG_DK, (hh + 1) * HG_DK)
        q = q_ref[0, pl.ds(r0, c_len), hs].astype(F32)
        k = kf_sc[:, hs]
        v_bf = v_ref[0, pl.ds(r0, c_len), hs]
        b = b_all[:, hs]
        b_end = b[c_len - 1:c_len, :]
        st = st_sc[hh]

        o = lax.dot_general((q * jnp.exp2(b)).astype(BF16), st.astype(BF16), nt, preferred_element_type=F32)

        e_rows = [b[(j + 1) * HG_SUB - 1:(j + 1) * HG_SUB, :] for j in range(n_sub)]
        e_full = jnp.concatenate([jnp.broadcast_to(e, (HG_SUB, LANES)) for e in e_rows], axis=0)
        k_rel = (k * jnp.exp2(e_full - b)).astype(BF16)
        q_rel = jnp.concatenate(
            [(q * jnp.exp2(jnp.minimum(b - e_rows[j], 0.0))).astype(BF16) for j in range(n_sub - 1)], axis=0)
        a_all = lax.dot_general(q_rel, k_rel, nt, preferred_element_type=F32)
        a_off = jnp.zeros((c_len, c_len), F32)
        for j in range(n_sub - 1):
            a_off = jnp.where(sub_c == j, a_all[j * c_len:(j + 1) * c_len], a_off)
        a_off = jnp.where(sub_r > sub_c, a_off, 0.0)
        o = o + jnp.dot(a_off.astype(BF16), v_bf, preferred_element_type=F32)

        pieces = []
        meta = []
        for i in range(n_sub):
            for s in range(HG_SUB):
                row = i * HG_SUB + s
                ks = kf_sc[pl.ds(row, 1), hs]
                bs = b_sc[pl.ds(row, 1), hs]
                for hf in range(HG_SUB // half):
                    t0 = hf * half
                    if t0 + half - 1 < s:
                        continue
                    rows = slice(i * HG_SUB + t0, i * HG_SUB + t0 + half)
                    w = q[rows] * ks * jnp.exp2(jnp.minimum(b[rows] - bs, 0.0))
                    if s > t0:
                        w = jnp.where(row8 + t0 >= s, w, 0.0)
                    pieces.append(w.astype(BF16))
                    meta.append((i, s, hf))
        n_pairs = len(pieces) // 2
        lhs = jnp.concatenate(
            [jnp.concatenate([pieces[2 * n], pieces[2 * n + 1]], axis=1) for n in range(n_pairs)], axis=0)
        sums = jnp.dot(lhs, ones_bd, preferred_element_type=F32)
        diag = [[jnp.zeros((half, LANES), F32) for _ in range(HG_SUB // half)] for _ in range(n_sub)]
        for n, (i, s, hf) in enumerate(meta):
            blk = sums[(n // 2) * half:(n // 2 + 1) * half, (n % 2) * LANES:(n % 2 + 1) * LANES]
            vs = vf_sc[pl.ds(i * HG_SUB + s, 1), hs]
            diag[i][hf] = diag[i][hf] + blk * vs
        o = o + jnp.concatenate([d for row in diag for d in row], axis=0)

        k_out = (k * jnp.exp2(b_end - b)).astype(BF16)
        st_sc[hh] = st * jnp.exp2(b_end) + lax.dot_general(
            v_bf, k_out, (((0,), (0,)), ((), ())), preferred_element_type=F32)

        ms = jnp.mean(o * o, axis=-1, keepdims=True)
        o_ref[pl.ds(r0, c_len), hs] = (o * lax.rsqrt(ms + RMS_EPS) * gn_ref[...]).astype(BF16)

    def chunk(c, carry):
        r0 = pl.multiple_of(c * c_len, c_len)
        b_all = jnp.dot(tri, g_ref[pl.ds(r0, c_len), :], precision=lax.Precision.HIGHEST,
                        preferred_element_type=F32)
        b_sc[...] = b_all
        kf_sc[...] = k_ref[0, pl.ds(r0, c_len), :].astype(F32)
        vf_sc[...] = v_ref[0, pl.ds(r0, c_len), :].astype(F32)
        for hh in range(hb):
            head(hh, r0, b_all)
        return carry

    lax.fori_loop(0, tc // c_len, chunk, 0)


def _hgrn2(main, g, gn, batch, lp):
    _, rp, d = main.shape
    tc = _largest_divisor(lp, (640, 512, 256, 128, 64))
    nt_ = lp // tc
    hb = 8
    wb = hb * HG_DK
    kern = functools.partial(_hgrn_kernel, tc=tc, hb=hb)

    def seg_spec(seg):
        return pl.BlockSpec((1, tc, wb), lambda b, h, t: (seg, b * nt_ + t, h))

    return pl.pallas_call(
        kern,
        grid=(batch, HG_HEADS // hb, nt_),
        in_specs=[
            seg_spec(3), seg_spec(4), seg_spec(5),
            pl.BlockSpec((tc, wb), lambda b, h, t: (b * nt_ + t, h)),
            pl.BlockSpec((1, HG_DK), lambda b, h, t: (0, 0)),
        ],
        out_specs=pl.BlockSpec((tc, wb), lambda b, h, t: (b * nt_ + t, h)),
        out_shape=jax.ShapeDtypeStruct((rp, d), BF16),
        scratch_shapes=[
            pltpu.VMEM((hb, HG_DK, HG_DK), F32),
            pltpu.VMEM((HG_CHUNK, wb), F32),
            pltpu.VMEM((HG_CHUNK, wb), F32),
            pltpu.VMEM((HG_CHUNK, wb), F32),
        ],
        compiler_params=pltpu.CompilerParams(
            dimension_semantics=("arbitrary", "arbitrary", "arbitrary"), vmem_limit_bytes=VMEM_LIMIT),
        name="hgrn2",
    )(main, main, main, g, gn)


def _outproj_kernel(ga_ref, gh_ref, og_ref, oa_ref, oh_ref, h_ref, w_ref, n2_ref, rw_ref, rb_ref,
                    h1_ref, u2_ref, ti_ref, gt_ref):
    tm = h_ref.shape[0]
    y = (ga_ref[0].astype(F32) * oa_ref[...].astype(F32)
         + gh_ref[0].astype(F32) * (oh_ref[...].astype(F32) * og_ref[0].astype(F32)))
    h1 = h_ref[...] + jnp.dot(y.astype(BF16), w_ref[...], preferred_element_type=F32)
    h1_ref[...] = h1
    ms = jnp.mean(h1 * h1, axis=-1, keepdims=True)
    u2 = h1 * lax.rsqrt(ms + RMS_EPS) * n2_ref[...]
    for s in range(ROW_TILE):
        u2_ref[pl.ds(s, tm, stride=ROW_TILE), :] = u2[:, s * LANES:(s + 1) * LANES]
    logits = jnp.dot(u2, rw_ref[...], precision=lax.Precision.HIGHEST,
                     preferred_element_type=F32) + rb_ref[...]
    lane = lax.broadcasted_iota(jnp.int32, logits.shape, 1)
    cur = logits
    vals, idxs = [], []
    for _ in range(TOP_K):
        mx = jnp.max(cur, axis=-1, keepdims=True)
        ix = jnp.min(jnp.where(cur == mx, lane, LANES), axis=-1, keepdims=True)
        vals.append(mx)
        idxs.append(ix)
        cur = jnp.where(lane == ix, -jnp.inf, cur)
    es = [jnp.exp(v - vals[0]) for v in vals]
    inv = 1.0 / (es[0] + es[1] + es[2] + es[3])
    ti = jnp.zeros(logits.shape, jnp.int32)
    gt = jnp.zeros(logits.shape, F32)
    for kk in range(TOP_K):
        ti = jnp.where(lane == kk, idxs[kk], ti)
        gt = jnp.where(lane == kk, es[kk] * inv, gt)
    ti_ref[...] = ti
    gt_ref[...] = gt


def _out_proj(main, o_att, o_hg, h_pad, w_out_bf, n2, rw_pad, rb_pad):
    rp, d = h_pad.shape
    tm = _largest_divisor(rp, (256, 128))

    def seg_spec(seg):
        return pl.BlockSpec((1, tm, d), lambda i: (seg, i, 0))

    row = pl.BlockSpec((tm, d), lambda i: (i, 0))
    small = pl.BlockSpec((tm, LANES), lambda i: (i, 0))
    return pl.pallas_call(
        _outproj_kernel,
        grid=(rp // tm,),
        in_specs=[
            seg_spec(7), seg_spec(8), seg_spec(6), row, row, row,
            pl.BlockSpec((d, d), lambda i: (0, 0)),
            pl.BlockSpec((1, d), lambda i: (0, 0)),
            pl.BlockSpec((d, LANES), lambda i: (0, 0)),
            pl.BlockSpec((1, LANES), lambda i: (0, 0)),
        ],
        out_specs=[row, pl.BlockSpec((tm * ROW_TILE, LANES), lambda i: (i, 0)), small, small],
        out_shape=[
            jax.ShapeDtypeStruct((rp, d), F32),
            jax.ShapeDtypeStruct((rp * ROW_TILE, LANES), F32),
            jax.ShapeDtypeStruct((rp, LANES), jnp.int32),
            jax.ShapeDtypeStruct((rp, LANES), F32),
        ],
        compiler_params=pltpu.CompilerParams(
            dimension_semantics=("arbitrary",), vmem_limit_bytes=VMEM_LIMIT),
        name="out_proj",
    )(main, main, main, o_att, o_hg, h_pad, w_out_bf, n2, rw_pad, rb_pad)


def _rank_kernel(ti_ref, rank_ref, cnt_ref, carry_sc, *, tm, lp):
    i = pl.program_id(0)

    @pl.when(i == 0)
    def _():
        carry_sc[...] = jnp.zeros(carry_sc.shape, F32)

    ti = ti_ref[...]
    lane = lax.broadcasted_iota(jnp.int32, (tm, LANES), 1)
    row = i * tm + lax.broadcasted_iota(jnp.int32, (tm, 1), 0)
    valid = lax.rem(row, lp) >= PAD
    hot = [(lane == ti[:, kk:kk + 1]) & valid for kk in range(TOP_K)]
    any_hot = hot[0] | hot[1] | hot[2] | hot[3]
    any_f = jnp.where(any_hot, 1.0, 0.0)
    rr = lax.broadcasted_iota(jnp.int32, (tm, tm), 0)
    cc = lax.broadcasted_iota(jnp.int32, (tm, tm), 1)
    strict = jnp.where(rr > cc, 1.0, 0.0).astype(BF16)
    base = carry_sc[...] + jnp.dot(strict, any_f.astype(BF16), preferred_element_type=F32)
    rank = jnp.zeros((tm, LANES), F32)
    for kk in range(TOP_K):
        rk = jnp.sum(jnp.where(hot[kk], base, 0.0), axis=-1, keepdims=True)
        rank = jnp.where(lane == kk, rk, rank)
    rank_ref[...] = rank.astype(jnp.int32)
    carry_sc[...] = carry_sc[...] + jnp.sum(any_f, axis=0, keepdims=True)
    cnt_ref[...] = carry_sc[...]


def _moe_rank(topi, lp):
    rp = topi.shape[0]
    tm = _largest_divisor(rp, (256, 128))
    kern = functools.partial(_rank_kernel, tm=tm, lp=lp)
    return pl.pallas_call(
        kern,
        grid=(rp // tm,),
        in_specs=[pl.BlockSpec((tm, LANES), lambda i: (i, 0))],
        out_specs=[pl.BlockSpec((tm, LANES), lambda i: (i, 0)), pl.BlockSpec((1, LANES), lambda i: (0, 0))],
        out_shape=[jax.ShapeDtypeStruct((rp, LANES), jnp.int32), jax.ShapeDtypeStruct((1, LANES), F32)],
        scratch_shapes=[pltpu.VMEM((1, LANES), F32)],
        compiler_params=pltpu.CompilerParams(dimension_semantics=("arbitrary",)),
        name="moe_rank",
    )(topi)


def _row_gather_start(src_hbm, dst, sem, tok, r):
    pltpu.make_async_copy(
        src_hbm.at[pl.ds(pl.multiple_of(tok * ROW_TILE, ROW_TILE), ROW_TILE), :],
        dst.at[pl.ds(pl.multiple_of(r * ROW_TILE, ROW_TILE), ROW_TILE), :],
        sem).start()


def _ffn_kernel(bexp_ref, nact_ref, tokc_ref, tokn_ref, u2_hbm, wgu_hbm, wdn_hbm, bgu_ref, bdn_ref, ys_ref,
                gbuf, ring, gu_sc, acc_sc, gsem, wsem, *, rb, d, dff, tk):
    i = pl.program_id(0)
    nact = nact_ref[0]
    slot = lax.rem(i, 2)
    nka = d // tk
    nkb = dff // tk
    n_chunks = 2 * nka + nkb
    n_ring = 3
    assert n_chunks % n_ring == 0 and dff == d

    def chunk_copy(blk, c, ring_slot):
        e = bexp_ref[blk]
        if c < 2 * nka:
            kt, half = divmod(c, 2)
            src = wgu_hbm.at[e, pl.ds(kt * tk, tk), pl.ds(half * dff, dff)]
        else:
            src = wdn_hbm.at[e, pl.ds((c - 2 * nka) * tk, tk), :]
        return pltpu.make_async_copy(src, ring.at[ring_slot], wsem.at[ring_slot])

    def issue_rows(tok_ref, sl):
        def body(r, carry):
            _row_gather_start(u2_hbm, gbuf.at[sl], gsem.at[sl], tok_ref[0, 0, r], r)
            return carry
        lax.fori_loop(0, rb, body, 0, unroll=8)

    @pl.when(i == 0)
    def _():
        issue_rows(tokc_ref, 0)
        chunk_copy(0, 0, 0).start()
        chunk_copy(0, 1, 1).start()

    @pl.when(i + 1 < nact)
    def _():
        issue_rows(tokn_ref, 1 - slot)

    @pl.when(i < nact)
    def _():
        pltpu.make_async_copy(gbuf.at[slot], gbuf.at[slot], gsem.at[slot]).wait()
        gu_sc[...] = jnp.broadcast_to(bgu_ref[0], gu_sc.shape)
        acc_sc[...] = jnp.broadcast_to(bdn_ref[0], acc_sc.shape)
        xk = None
        for c in range(n_chunks):
            rs = c % n_ring
            chunk_copy(i, c, rs).wait()
            nxt = c + n_ring - 1
            if nxt < n_chunks:
                chunk_copy(i, nxt, nxt % n_ring).start()
            else:
                chunk_copy(i + 1, nxt - n_chunks, nxt % n_ring).start()
            w = ring[rs].astype(BF16)
            if c < 2 * nka:
                kt, half = divmod(c, 2)
                if half == 0:
                    per = tk // LANES
                    xk = jnp.concatenate(
                        [gbuf[slot, pl.ds(kt * per + s, rb, stride=ROW_TILE), :].astype(BF16)
                         for s in range(per)], axis=1)
                cols = slice(half * dff, (half + 1) * dff)
                gu_sc[:, cols] += jnp.dot(xk, w, preferred_element_type=F32)
            else:
                kt = c - 2 * nka
                xg = jnp.minimum(gu_sc[:, kt * tk:(kt + 1) * tk], SWIGLU_LIMIT)
                xl = jnp.clip(gu_sc[:, dff + kt * tk:dff + (kt + 1) * tk], -SWIGLU_LIMIT, SWIGLU_LIMIT)
                act = (xg * _sigmoid(SWIGLU_ALPHA * xg) * (xl + 1.0)).astype(BF16)
                acc_sc[...] += jnp.dot(act, w, preferred_element_type=F32)
        for s in range(ROW_TILE):
            ys_ref[pl.ds(s, rb, stride=ROW_TILE), :] = acc_sc[:, s * LANES:(s + 1) * LANES]

    @pl.when(i == nact)
    def _():
        chunk_copy(i, 0, 0).wait()
        chunk_copy(i, 1, 1).wait()

    @pl.when(i >= nact)
    def _():
        ys_ref[...] = jnp.zeros(ys_ref.shape, F32)


def _moe_ffn(block_exp, nact, slot_tok, u2_rows, w_gu, b_gu, w_dn, b_dn, rb):
    n_e, d, f2 = w_gu.shape
    dff = f2 // 2
    nblk = slot_tok.shape[0]
    tk = 512
    kern = functools.partial(_ffn_kernel, rb=rb, d=d, dff=dff, tk=tk)
    grid_spec = pltpu.PrefetchScalarGridSpec(
        num_scalar_prefetch=2,
        grid=(nblk,),
        in_specs=[
            pl.BlockSpec((1, 1, rb), lambda i, be, na: (i, 0, 0), memory_space=pltpu.SMEM),
            pl.BlockSpec((1, 1, rb), lambda i, be, na: (jnp.minimum(i + 1, nblk - 1), 0, 0),
                         memory_space=pltpu.SMEM),
            pl.BlockSpec(memory_space=pl.ANY),
            pl.BlockSpec(memory_space=pl.ANY),
            pl.BlockSpec(memory_space=pl.ANY),
            pl.BlockSpec((1, 1, f2), lambda i, be, na: (be[i], 0, 0)),
            pl.BlockSpec((1, 1, d), lambda i, be, na: (be[i], 0, 0)),
        ],
        out_specs=pl.BlockSpec((rb * ROW_TILE, LANES), lambda i, be, na: (i, 0)),
        scratch_shapes=[
            pltpu.VMEM((2, rb * ROW_TILE, LANES), F32),
            pltpu.VMEM((3, tk, d), F32),
            pltpu.VMEM((rb, f2), F32),
            pltpu.VMEM((rb, d), F32),
            pltpu.SemaphoreType.DMA((2,)),
            pltpu.SemaphoreType.DMA((3,)),
        ],
    )
    return pl.pallas_call(
        kern,
        grid_spec=grid_spec,
        out_shape=jax.ShapeDtypeStruct((nblk * rb * ROW_TILE, LANES), F32),
        compiler_params=pltpu.CompilerParams(
            dimension_semantics=("arbitrary",), vmem_limit_bytes=VMEM_LIMIT),
        name="moe_ffn",
    )(block_exp, nact, slot_tok, slot_tok, u2_rows, w_gu, w_dn, b_gu.reshape(n_e, 1, f2),
      b_dn.reshape(n_e, 1, d))


def _combine_kernel(dc_ref, dn_ref, ys_hbm, gt_ref, h1_ref, o_ref, cbuf, sem, *, tm, n_steps, n_i):
    n = pl.program_id(0) * n_i + pl.program_id(1)
    slot = lax.rem(n, 2)

    def issue(dref, sl):
        def body(r, carry):
            for kk in range(TOP_K):
                _row_gather_start(ys_hbm, cbuf.at[sl, kk], sem.at[sl], dref[0, 0, r * TOP_K + kk], r)
            return carry
        lax.fori_loop(0, tm, body, 0, unroll=4)

    @pl.when(n == 0)
    def _():
        issue(dc_ref, 0)

    @pl.when(n + 1 < n_steps)
    def _():
        issue(dn_ref, 1 - slot)

    pltpu.make_async_copy(cbuf.at[slot], cbuf.at[slot], sem.at[slot]).wait()
    gates = [jnp.broadcast_to(gt_ref[:, kk:kk + 1], (tm, LANES)) for kk in range(TOP_K)]
    for s in range(ROW_TILE):
        sl = slice(s * LANES, (s + 1) * LANES)
        y = h1_ref[:, sl]
        for kk in range(TOP_K):
            y = y + gates[kk] * cbuf[slot, kk, pl.ds(s, tm, stride=ROW_TILE), :]
        o_ref[0, :, sl] = y


def _combine(dest_x, ys_rows, gate, h1, batch, seq, lp):
    rp, d = h1.shape
    tm = BLOCK
    n_i = seq // tm
    n_steps = batch * n_i
    nlp = lp // tm
    kern = functools.partial(_combine_kernel, tm=tm, n_steps=n_steps, n_i=n_i)
    return pl.pallas_call(
        kern,
        grid=(batch, n_i),
        in_specs=[
            pl.BlockSpec((1, 1, tm * TOP_K), lambda b, i: (b * n_i + i, 0, 0), memory_space=pltpu.SMEM),
            pl.BlockSpec((1, 1, tm * TOP_K), lambda b, i: (jnp.minimum(b * n_i + i + 1, n_steps - 1), 0, 0),
                         memory_space=pltpu.SMEM),
            pl.BlockSpec(memory_space=pl.ANY),
            pl.BlockSpec((tm, LANES), lambda b, i: (b * nlp + 1 + i, 0)),
            pl.BlockSpec((tm, d), lambda b, i: (b * nlp + 1 + i, 0)),
        ],
        out_specs=pl.BlockSpec((1, tm, d), lambda b, i: (b, i, 0)),
        out_shape=jax.ShapeDtypeStruct((batch, seq, d), F32),
        scratch_shapes=[
            pltpu.VMEM((2, TOP_K, tm * ROW_TILE, LANES), F32),
            pltpu.SemaphoreType.DMA((2,)),
        ],
        compiler_params=pltpu.CompilerParams(
            dimension_semantics=("arbitrary", "arbitrary"), vmem_limit_bytes=VMEM_LIMIT),
        name="moe_combine",
    )(dest_x, dest_x, ys_rows, gate, h1)


def kernel(x, meta_tokens, rel_bias, lb_logits, norm1, w_in, q_norm, k_norm, diff_lambda, diff_subln,
           hgrn_norm, w_out, norm2, router_w, router_b, w_gate_up, b_gate_up, w_down, b_down):
    batch, seq, d = x.shape
    assert norm1.shape[0] == 1, "single-layer block"
    assert d == ATT_HEADS * ATT_VD == HG_HEADS * HG_DK == ROW_TILE * LANES and seq % BLOCK == 0
    lp = seq + BLOCK
    rp = batch * lp

    first = jnp.concatenate([jnp.zeros((PAD, d), x.dtype), meta_tokens.astype(x.dtype)], axis=0)
    h_pad = jnp.concatenate([jnp.broadcast_to(first[None], (batch, BLOCK, d)), x], axis=1).reshape(rp, d)

    lb = jax.nn.softmax(lb_logits.astype(F32), axis=0)[0]
    lv = diff_lambda[0].astype(F32)
    lam = (jnp.exp(jnp.sum(lv[0] * lv[1])) - jnp.exp(jnp.sum(lv[2] * lv[3])) + LAMBDA_INIT).reshape(1, 1)
    n_grp = d // ATT_HD
    gains = jnp.concatenate([jnp.tile(q_norm[0].astype(F32) * (ATT_HD ** -0.5 * LOG2E), n_grp),
                             jnp.tile(k_norm[0].astype(F32), n_grp),
                             jnp.ones(((N_SEG - 2) * d,), F32)])
    lbs = jnp.concatenate([jnp.zeros((4 * d,), F32), lb, jnp.zeros((4 * d,), F32)])
    colp = jnp.stack([gains, lbs])

    main, g = _in_proj(h_pad, norm1.astype(F32), w_in[0].astype(BF16), colp)

    blk = _largest_divisor(lp, (640, 512, 384, 256, 128))
    pat_d0, pat_d1 = _bias_patterns(rel_bias)
    subln = (diff_subln[0].astype(F32) * (1.0 - LAMBDA_INIT)).reshape(1, ATT_VD)
    o_att = _diff_attention(main, lam, pat_d0, pat_d1, subln, batch, lp, blk)

    o_hg = _hgrn2(main, g, hgrn_norm.astype(F32).reshape(1, HG_DK), batch, lp)

    rw_pad = jnp.pad(router_w[0].astype(F32), ((0, 0), (0, LANES - N_EXPERTS)))
    rb_pad = jnp.pad(router_b[0].astype(F32), (0, LANES - N_EXPERTS), constant_values=NEG_INF).reshape(1, LANES)
    h1, u2_rows, topi, gate = _out_proj(main, o_att, o_hg, h_pad, w_out[0].astype(BF16), norm2.astype(F32),
                                        rw_pad, rb_pad)

    rank, counts = _moe_rank(topi, lp)

    rb = 512
    n_assign = batch * (seq + N_META) * TOP_K
    nblk = n_assign // rb + N_EXPERTS + 1
    cnt = counts[0, :N_EXPERTS].astype(jnp.int32)
    padded = (cnt + rb - 1) // rb * rb
    pend = jnp.cumsum(padded)
    pstart = pend - padded
    rows = jnp.arange(rp, dtype=jnp.int32)
    valid = (rows % lp) >= PAD
    ti4 = topi[:, :TOP_K]
    dest = pstart[ti4] + rank[:, :TOP_K]
    dest_s = jnp.where(valid[:, None], dest, nblk * rb)
    slot_tok = jnp.zeros((nblk * rb,), jnp.int32).at[dest_s.reshape(-1)].set(
        jnp.repeat(rows, TOP_K), mode="drop")
    nact = (pend[-1] // rb).astype(jnp.int32)
    blk_ids = jnp.minimum(jnp.arange(nblk, dtype=jnp.int32), nact - 1)
    block_exp = jnp.minimum(jnp.sum((pend[None, :] <= (blk_ids * rb)[:, None]).astype(jnp.int32), axis=1),
                            N_EXPERTS - 1)

    ys_rows = _moe_ffn(block_exp, nact.reshape(1), slot_tok.reshape(nblk, 1, rb), u2_rows,
                       w_gate_up[0], b_gate_up[0].astype(F32), w_down[0], b_down[0].astype(F32), rb)

    dest_x = dest.reshape(batch, lp, TOP_K)[:, BLOCK:, :].reshape(batch * seq // BLOCK, 1, BLOCK * TOP_K)
    return _combine(dest_x, ys_rows, gate, h1, batch, seq, lp)
```

```python
import functools
import math

import jax
import jax.numpy as jnp
from jax import lax
from jax.experimental import pallas as pl
from jax.experimental.pallas import tpu as pltpu

N_META = 16
BLOCK = 128
PAD = BLOCK - N_META

ATT_HEADS = 8
ATT_HD = 128
ATT_VD = 2 * ATT_HD

HG_HEADS = 16
HG_DK = 128
HG_CHUNK = 64
HG_SUB = 16

REL_BUCKETS = 32
REL_MAX_DIST = 128

N_EXPERTS = 32
TOP_K = 4
SWIGLU_LIMIT = 7.0
SWIGLU_ALPHA = 1.702

RMS_EPS = 1e-6
NEG_INF = -1e30
LAMBDA_INIT = 0.8 - 0.6 * math.exp(-0.3 * 0)
LOG2E = 1.0 / math.log(2.0)

N_SEG = 9
LANES = 128
SUBLANES = 8
ROW_TILE = 16
VMEM_LIMIT = 56 * 1024 * 1024

F32 = jnp.float32
BF16 = jnp.bfloat16


def _largest_divisor(n, candidates):
    for c in candidates:
        if n % c == 0:
            return c
    raise ValueError(f"no tile in {candidates} divides {n}")


def _sigmoid(x):
    return 1.0 / (1.0 + jnp.exp(-x))


def _inproj_kernel(x_ref, n1_ref, w_ref, cp_ref, main_ref, g_ref, u_ref, *, tq):
    seg = pl.program_id(1)

    @pl.when(seg == 0)
    def _():
        xf = x_ref[...]
        ms = jnp.mean(xf * xf, axis=-1, keepdims=True)
        u_ref[...] = (xf * lax.rsqrt(ms + RMS_EPS) * n1_ref[...]).astype(BF16)

    def slabs(epilogue):
        for q in range(w_ref.shape[1] // tq):
            cs = slice(q * tq, (q + 1) * tq)
            epilogue(jnp.dot(u_ref[...], w_ref[:, cs], preferred_element_type=F32), cs)

    @pl.when(seg <= 1)
    def _():
        def epilogue(acc, cs):
            for gi in range(tq // LANES):
                sl = slice(cs.start + gi * LANES, cs.start + (gi + 1) * LANES)
                y = acc[:, gi * LANES:(gi + 1) * LANES]
                ms = jnp.mean(y * y, axis=-1, keepdims=True)
                main_ref[0, :, sl] = (y * lax.rsqrt(ms + RMS_EPS) * cp_ref[0:1, sl]).astype(BF16)
        slabs(epilogue)

    @pl.when((seg == 2) | (seg == 5))
    def _():
        def epilogue(acc, cs):
            main_ref[0, :, cs] = acc.astype(BF16)
        slabs(epilogue)

    @pl.when((seg == 3) | (seg == 6))
    def _():
        def epilogue(acc, cs):
            main_ref[0, :, cs] = (acc * _sigmoid(acc)).astype(BF16)
        slabs(epilogue)

    @pl.when(seg == 4)
    def _():
        def epilogue(acc, cs):
            lb = cp_ref[1:2, cs]
            sg = _sigmoid(acc)
            g_ref[:, cs] = jnp.log(lb + (1.0 - lb) * sg) * LOG2E
            main_ref[0, :, cs] = ((1.0 - lb) * (1.0 - sg)).astype(BF16)
        slabs(epilogue)

    @pl.when(seg >= 7)
    def _():
        def epilogue(acc, cs):
            main_ref[0, :, cs] = _sigmoid(acc).astype(BF16)
        slabs(epilogue)


def _in_proj(h_pad, n1, w_in_bf, colp):
    rp, d = h_pad.shape
    tm = _largest_divisor(rp, (512, 256, 128))
    kern = functools.partial(_inproj_kernel, tq=512)
    return pl.pallas_call(
        kern,
        grid=(rp // tm, N_SEG),
        in_specs=[
            pl.BlockSpec((tm, d), lambda i, j: (i, 0)),
            pl.BlockSpec((1, d), lambda i, j: (0, 0)),
            pl.BlockSpec((d, d), lambda i, j: (0, j)),
            pl.BlockSpec((2, d), lambda i, j: (0, j)),
        ],
        out_specs=[
            pl.BlockSpec((1, tm, d), lambda i, j: (j, i, 0)),
            pl.BlockSpec((tm, d), lambda i, j: (i, 0)),
        ],
        out_shape=[
            jax.ShapeDtypeStruct((N_SEG, rp, d), BF16),
            jax.ShapeDtypeStruct((rp, d), F32),
        ],
        scratch_shapes=[pltpu.VMEM((tm, d), BF16)],
        compiler_params=pltpu.CompilerParams(
            dimension_semantics=("arbitrary", "arbitrary"), vmem_limit_bytes=VMEM_LIMIT),
        name="in_proj",
    )(h_pad, n1, w_in_bf, colp)


def _t5_bucket(rel):
    n = jnp.maximum(rel, 0)
    max_exact = REL_BUCKETS // 2
    nf = jnp.maximum(n, 1).astype(F32)
    large = max_exact + (jnp.log(nf / max_exact) / math.log(REL_MAX_DIST / max_exact)
                         * (REL_BUCKETS - max_exact)).astype(jnp.int32)
    large = jnp.minimum(large, REL_BUCKETS - 1)
    return jnp.where(n < max_exact, n, large)


def _bias_patterns(rel_bias):
    rb =(rel_bias.astype(F32) - rel_bias.astype(F32)[REL_BUCKETS - 1][None, :]) * LOG2E
    by_dist = rb[_t5_bucket(jnp.arange(2 * BLOCK, dtype=jnp.int32))].T
    r = jnp.arange(BLOCK, dtype=jnp.int32)[:, None]
    c = jnp.arange(BLOCK, dtype=jnp.int32)[None, :]
    d0 = jnp.where((r >= c)[None], by_dist[:, jnp.maximum(r - c, 0)], NEG_INF)
    d1 = by_dist[:, BLOCK + r - c]
    return d0, d1


def _tile_bias(d0, d1, nb, diagonal):
    zero = jnp.zeros((BLOCK, BLOCK), F32)
    neg = jnp.full((BLOCK, BLOCK), NEG_INF, F32)
    rows = []
    for a in range(nb):
        if diagonal:
            blocks = [d0 if c == a else d1 if c == a - 1 else neg if c > a else zero for c in range(nb)]
        else:
            blocks = [d1 if (a == 0 and c == nb - 1) else zero for c in range(nb)]
        rows.append(jnp.concatenate(blocks, axis=1))
    return jnp.concatenate(rows, axis=0)


def _attn_kernel(qi_tab, ki_tab, lam_ref, q_ref, k_ref, v_ref, d0_ref, d1_ref, sg_ref, o_ref,
                 m_sc, l_sc, acc_sc, *, bq, bk, hpb):
    p = pl.program_id(2)
    qi = qi_tab[p]
    ki = ki_tab[p]
    nlb = bk // LANES

    @pl.when(ki == 0)
    def _():
        m_sc[...] = jnp.full(m_sc.shape, NEG_INF, BF16).astype(F32)
        l_sc[...] = jnp.zeros(l_sc.shape, F32)
        acc_sc[...] = jnp.zeros(acc_sc.shape, F32)

    def scores(hh, m):
        sl = slice(hh * ATT_VD + m * ATT_HD, hh * ATT_VD + (m + 1) * ATT_HD)
        return lax.dot_general(q_ref[0, :, sl], k_ref[0, :, sl], (((1,), (1,)), ((), ())),
                               preferred_element_type=F32)

    def update(hh, m, s):
        c_idx = 2 * hh + m
        sb = s.astype(BF16)
        lane_blocks = [sb[:, c * LANES:(c + 1) * LANES] for c in range(nlb)]
        m_old = m_sc[c_idx]
        blk_max = functools.reduce(jnp.maximum, lane_blocks).astype(F32)
        m_new = jnp.maximum(m_old, jnp.max(blk_max, axis=-1, keepdims=True))
        alpha = jnp.exp2(m_old - m_new)
        m_b = m_new.astype(BF16)
        pbs = [jnp.exp2(lb - m_b) for lb in lane_blocks]
        l_sc[c_idx] = alpha * l_sc[c_idx] + functools.reduce(jnp.add, pbs).astype(F32)
        acc_sc[c_idx] = jnp.concatenate([alpha] * (ATT_VD // LANES), axis=1) * acc_sc[c_idx] + jnp.dot(
            jnp.concatenate(pbs, axis=1), v_ref[0, :, hh * ATT_VD:(hh + 1) * ATT_VD],
            preferred_element_type=F32)
        m_sc[c_idx] = m_new

    def key_valid():
        col = ki * bk + lax.broadcasted_iota(jnp.int32, (1, bk), 1)
        return col >= PAD

    def sweep(bias_of, masked):
        ok = key_valid() if masked else None
        for hh in range(hpb):
            bias = bias_of(hh)
            for m in range(2):
                s = scores(hh, m)
                if bias is not None:
                    s = s + bias
                if masked:
                    s = jnp.where(ok, s, NEG_INF)
                update(hh, m, s)

    @pl.when(ki == qi)
    def _():
        sweep(lambda hh: _tile_bias(d0_ref[hh], d1_ref[hh], bk // BLOCK, True), True)
        lam = lam_ref[0, 0]
        for hh in range(hpb):
            l0 = jnp.sum(l_sc[2 * hh], axis=-1, keepdims=True)
            l1 = jnp.sum(l_sc[2 * hh + 1], axis=-1, keepdims=True)
            o = acc_sc[2 * hh] / l0 - lam * (acc_sc[2 * hh + 1] / l1)
            ms = jnp.mean(o * o, axis=-1, keepdims=True)
            o_ref[:, hh * ATT_VD:(hh + 1) * ATT_VD] = (o * lax.rsqrt(ms + RMS_EPS) * sg_ref[...]).astype(BF16)

    @pl.when(ki == qi - 1)
    def _():
        sweep(lambda hh: _tile_bias(d0_ref[hh], d1_ref[hh], bk // BLOCK, False), True)

    @pl.when((ki < qi - 1) & (ki == 0))
    def _():
        sweep(lambda hh: None, True)

    @pl.when((ki < qi - 1) & (ki > 0))
    def _():
        sweep(lambda hh: None, False)


def _diff_attention(main, lam, pat_d0, pat_d1, subln, batch, lp, blk):
    _, rp, d = main.shape
    nb = lp // blk
    hpb = 2
    wb = hpb * ATT_VD
    pairs = [(qi, ki) for qi in range(nb) for ki in range(qi + 1)]
    qi_tab = jnp.asarray([a for a, _ in pairs], jnp.int32)
    ki_tab = jnp.asarray([b for _, b in pairs], jnp.int32)
    kern = functools.partial(_attn_kernel, bq=blk, bk=blk, hpb=hpb)
    grid_spec = pltpu.PrefetchScalarGridSpec(
        num_scalar_prefetch=2,
        grid=(batch, ATT_HEADS // hpb, len(pairs)),
        in_specs=[
            pl.BlockSpec(memory_space=pltpu.SMEM),
            pl.BlockSpec((1, blk, wb), lambda b, h, p, qt, kt: (0, b * nb + qt[p], h)),
            pl.BlockSpec((1, blk, wb), lambda b, h, p, qt, kt: (1, b * nb + kt[p], h)),
            pl.BlockSpec((1, blk, wb), lambda b, h, p, qt, kt: (2, b * nb + kt[p], h)),
            pl.BlockSpec((hpb, BLOCK, BLOCK), lambda b, h, p, qt, kt: (h, 0, 0)),
            pl.BlockSpec((hpb, BLOCK, BLOCK), lambda b, h, p, qt, kt: (h, 0, 0)),
            pl.BlockSpec((1, ATT_VD), lambda b, h, p, qt, kt: (0, 0)),
        ],
        out_specs=pl.BlockSpec((blk, wb), lambda b, h, p, qt, kt: (b * nb + qt[p], h)),
        scratch_shapes=[
            pltpu.VMEM((2 * hpb, blk, LANES), F32),
            pltpu.VMEM((2 * hpb, blk, LANES), F32),
            pltpu.VMEM((2 * hpb, blk, ATT_VD), F32),
        ],
    )
    return pl.pallas_call(
        kern,
        grid_spec=grid_spec,
        out_shape=jax.ShapeDtypeStruct((rp, d), BF16),
        compiler_params=pltpu.CompilerParams(
            dimension_semantics=("arbitrary", "arbitrary", "arbitrary"), vmem_limit_bytes=VMEM_LIMIT),
        name="diff_attn",
    )(qi_tab, ki_tab, lam, main, main, main, pat_d0, pat_d1, subln)


def _hgrn_kernel(q_ref, k_ref, v_ref, g_ref, gn_ref, o_ref, st_sc, b_sc, kf_sc, vf_sc, *, tc, hb):
    c_len = HG_CHUNK
    n_sub = HG_CHUNK // HG_SUB
    half = SUBLANES

    @pl.when(pl.program_id(2) == 0)
    def _():
        st_sc[...] = jnp.zeros(st_sc.shape, F32)

    ri = lax.broadcasted_iota(jnp.int32, (c_len, c_len), 0)
    ci = lax.broadcasted_iota(jnp.int32, (c_len, c_len), 1)
    tri = (ri >= ci).astype(F32)
    sub_r = ri // HG_SUB
    sub_c = ci // HG_SUB
    r2 = lax.broadcasted_iota(jnp.int32, (2 * LANES, 2 * LANES), 0) // LANES
    c2 = lax.broadcasted_iota(jnp.int32, (2 * LANES, 2 * LANES), 1) // LANES
    ones_bd = (r2 == c2).astype(BF16)
    row8 = lax.broadcasted_iota(jnp.int32, (half, LANES), 0)
    nt = (((1,), (1,)), ((), ()))

    def head(hh, r0, b_all):
        hs = slice(hh * HG_DK, (hh + 1) * HG_DK)
        q = q_ref[0, pl.ds(r0, c_len), hs].astype(F32)
        k = kf_sc[:, hs]
        v_bf = v_ref[0, pl.ds(r0, c_len), hs]
        b = b_all[:, hs]
        b_end = b[c_len - 1:c_len, :]
        st = st_sc[hh]

        o = lax.dot_general((q * jnp.exp2(b)).astype(BF16), st.astype(BF16), nt, preferred_element_type=F32)

        e_rows = [b[(j + 1) * HG_SUB - 1:(j + 1) * HG_SUB, :] for j in range(n_sub)]
        e_full = jnp.concatenate([jnp.broadcast_to(e, (HG_SUB, LANES)) for e in e_rows], axis=0)
        k_rel = (k * jnp.exp2(e_full - b)).astype(BF16)
        q_rel = jnp.concatenate(
            [(q * jnp.exp2(jnp.minimum(b - e_rows[j], 0.0))).astype(BF16) for j in range(n_sub - 1)], axis=0)
        a_all = lax.dot_general(q_rel, k_rel, nt, preferred_element_type=F32)
        a_off = jnp.zeros((c_len, c_len), F32)
        for j in range(n_sub - 1):
            a_off = jnp.where(sub_c == j, a_all[j * c_len:(j + 1) * c_len], a_off)
        a_off = jnp.where(sub_r > sub_c, a_off, 0.0)
        o = o + jnp.dot(a_off.astype(BF16), v_bf, preferred_element_type=F32)

        pieces = []
        meta = []
        for i in range(n_sub):
            for s in range(HG_SUB):
                row = i * HG_SUB + s
                ks = kf_sc[pl.ds(row, 1), hs]
                bs = b_sc[pl.ds(row, 1), hs]
                for hf in range(HG_SUB // half):
                    t0 = hf * half
                    if t0 + half - 1 < s:
                        continue
                    rows = slice(i * HG_SUB + t0, i * HG_SUB + t0 + half)
                    w = q[rows] * ks * jnp.exp2(jnp.minimum(b[rows] - bs, 0.0))
                    if s > t0:
                        w = jnp.where(row8 + t0 >= s, w, 0.0)
                    pieces.append(w.astype(BF16))
                    meta.append((i, s, hf))
        n_pairs = len(pieces) // 2
        lhs = jnp.concatenate(
            [jnp.concatenate([pieces[2 * n], pieces[2 * n + 1]], axis=1) for n in range(n_pairs)], axis=0)
        sums = jnp.dot(lhs, ones_bd, preferred_element_type=F32)
        diag = [[jnp.zeros((half, LANES), F32) for _ in range(HG_SUB // half)] for _ in range(n_sub)]
        for n, (i, s, hf) in enumerate(meta):
            blk = sums[(n // 2) * half:(n // 2 + 1) * half, (n % 2) * LANES:(n % 2 + 1) * LANES]
            vs = vf_sc[pl.ds(i * HG_SUB + s, 1), hs]
            diag[i][hf] = diag[i][hf] + blk * vs
        o = o + jnp.concatenate([d for row in diag for d in row], axis=0)

        k_out = (k * jnp.exp2(b_end - b)).astype(BF16)
        st_sc[hh] = st * jnp.exp2(b_end) + lax.dot_general(
            v_bf, k_out, (((0,), (0,)), ((), ())), preferred_element_type=F32)

        ms = jnp.mean(o * o, axis=-1, keepdims=True)
        o_ref[pl.ds(r0, c_len), hs] = (o * lax.rsqrt(ms + RMS_EPS) * gn_ref[...]).astype(BF16)

    def chunk(c, carry):
        r0 = pl.multiple_of(c * c_len, c_len)
        b_all = jnp.dot(tri, g_ref[pl.ds(r0, c_len), :], precision=lax.Precision.HIGHEST,
                        preferred_element_type=F32)
        b_sc[...] = b_all
        kf_sc[...] = k_ref[0, pl.ds(r0, c_len), :].astype(F32)
        vf_sc[...] = v_ref[0, pl.ds(r0, c_len), :].astype(F32)
        for hh in range(hb):
            head(hh, r0, b_all)
        return carry

    lax.fori_loop(0, tc // c_len, chunk, 0)


def _hgrn2(main, g, gn, batch, lp):
    _, rp, d = main.shape
    tc = _largest_divisor(lp, (640, 512, 256, 128, 64))
    nt_ = lp // tc
    hb = 8
    wb = hb * HG_DK
    kern = functools.partial(_hgrn_kernel, tc=tc, hb=hb)

    def seg_spec(seg):
        return pl.BlockSpec((1, tc, wb), lambda b, h, t: (seg, b * nt_ + t, h))

    return pl.pallas_call(
        kern,
        grid=(batch, HG_HEADS // hb, nt_),
        in_specs=[
            seg_spec(3), seg_spec(4), seg_spec(5),
            pl.BlockSpec((tc, wb), lambda b, h, t: (b * nt_ + t, h)),
            pl.BlockSpec((1, HG_DK), lambda b, h, t: (0, 0)),
        ],
        out_specs=pl.BlockSpec((tc, wb), lambda b, h, t: (b * nt_ + t, h)),
        out_shape=jax.ShapeDtypeStruct((rp, d), BF16),
        scratch_shapes=[
            pltpu.VMEM((hb, HG_DK, HG_DK), F32),
            pltpu.VMEM((HG_CHUNK, wb), F32),
            pltpu.VMEM((HG_CHUNK, wb), F32),
            pltpu.VMEM((HG_CHUNK, wb), F32),
        ],
        compiler_params=pltpu.CompilerParams(
            dimension_semantics=("arbitrary", "arbitrary", "arbitrary"), vmem_limit_bytes=VMEM_LIMIT),
        name="hgrn2",
    )(main, main, main, g, gn)


def _outproj_kernel(ga_ref, gh_ref, og_ref, oa_ref, oh_ref, h_ref, w_ref, n2_ref, rw_ref, rb_ref,
                    h1_ref, u2_ref, ti_ref, gt_ref):
    tm = h_ref.shape[0]
    y = (ga_ref[0].astype(F32) * oa_ref[...].astype(F32)
         + gh_ref[0].astype(F32) * (oh_ref[...].astype(F32) * og_ref[0].astype(F32)))
    h1 = h_ref[...] + jnp.dot(y.astype(BF16), w_ref[...], preferred_element_type=F32)
    h1_ref[...] = h1
    ms = jnp.mean(h1 * h1, axis=-1, keepdims=True)
    u2 = h1 * lax.rsqrt(ms + RMS_EPS) * n2_ref[...]
    for s in range(ROW_TILE):
        u2_ref[pl.ds(s, tm, stride=ROW_TILE), :] = u2[:, s * LANES:(s + 1) * LANES]
    logits = jnp.dot(u2, rw_ref[...], precision=lax.Precision.HIGHEST,
                     preferred_element_type=F32) + rb_ref[...]
    lane = lax.broadcasted_iota(jnp.int32, logits.shape, 1)
    cur = logits
    vals, idxs = [], []
    for _ in range(TOP_K):
        mx = jnp.max(cur, axis=-1, keepdims=True)
        ix = jnp.min(jnp.where(cur == mx, lane, LANES), axis=-1, keepdims=True)
        vals.append(mx)
        idxs.append(ix)
        cur = jnp.where(lane == ix, -jnp.inf, cur)
    es = [jnp.exp(v - vals[0]) for v in vals]
    inv = 1.0 / (es[0] + es[1] + es[2] + es[3])
    ti = jnp.zeros(logits.shape, jnp.int32)
    gt = jnp.zeros(logits.shape, F32)
    for kk in range(TOP_K):
        ti = jnp.where(lane == kk, idxs[kk], ti)
        gt = jnp.where(lane == kk, es[kk] * inv, gt)
    ti_ref[...] = ti
    gt_ref[...] = gt


def _out_proj(main, o_att, o_hg, h_pad, w_out_bf, n2, rw_pad, rb_pad):
    rp, d = h_pad.shape
    tm = _largest_divisor(rp, (256, 128))

    def seg_spec(seg):
        return pl.BlockSpec((1, tm, d), lambda i: (seg, i, 0))

    row = pl.BlockSpec((tm, d), lambda i: (i, 0))
    small = pl.BlockSpec((tm, LANES), lambda i: (i, 0))
    return pl.pallas_call(
        _outproj_kernel,
        grid=(rp // tm,),
        in_specs=[
            seg_spec(7), seg_spec(8), seg_spec(6), row, row, row,
            pl.BlockSpec((d, d), lambda i: (0, 0)),
            pl.BlockSpec((1, d), lambda i: (0, 0)),
            pl.BlockSpec((d, LANES), lambda i: (0, 0)),
            pl.BlockSpec((1, LANES), lambda i: (0, 0)),
        ],
        out_specs=[row, pl.BlockSpec((tm * ROW_TILE, LANES), lambda i: (i, 0)), small, small],
        out_shape=[
            jax.ShapeDtypeStruct((rp, d), F32),
            jax.ShapeDtypeStruct((rp * ROW_TILE, LANES), F32),
            jax.ShapeDtypeStruct((rp, LANES), jnp.int32),
            jax.ShapeDtypeStruct((rp, LANES), F32),
        ],
        compiler_params=pltpu.CompilerParams(
            dimension_semantics=("arbitrary",), vmem_limit_bytes=VMEM_LIMIT),
        name="out_proj",
    )(main, main, main, o_att, o_hg, h_pad, w_out_bf, n2, rw_pad, rb_pad)


def _rank_kernel(ti_ref, rank_ref, cnt_ref, carry_sc, *, tm, lp):
    i = pl.program_id(0)

    @pl.when(i == 0)
    def _():
        carry_sc[...] = jnp.zeros(carry_sc.shape, F32)

    ti = ti_ref[...]
    lane = lax.broadcasted_iota(jnp.int32, (tm, LANES), 1)
    row = i * tm + lax.broadcasted_iota(jnp.int32, (tm, 1), 0)
    valid = lax.rem(row, lp) >= PAD
    hot = [(lane == ti[:, kk:kk + 1]) & valid for kk in range(TOP_K)]
    any_hot = hot[0] | hot[1] | hot[2] | hot[3]
    any_f = jnp.where(any_hot, 1.0, 0.0)
    rr = lax.broadcasted_iota(jnp.int32, (tm, tm), 0)
    cc = lax.broadcasted_iota(jnp.int32, (tm, tm), 1)
    strict = jnp.where(rr > cc, 1.0, 0.0).astype(BF16)
    base = carry_sc[...] + jnp.dot(strict, any_f.astype(BF16), preferred_element_type=F32)
    rank = jnp.zeros((tm, LANES), F32)
    for kk in range(TOP_K):
        rk = jnp.sum(jnp.where(hot[kk], base, 0.0), axis=-1, keepdims=True)
        rank = jnp.where(lane == kk, rk, rank)
    rank_ref[...] = rank.astype(jnp.int32)
    carry_sc[...] = carry_sc[...] + jnp.sum(any_f, axis=0, keepdims=True)
    cnt_ref[...] = carry_sc[...]


def _moe_rank(topi, lp):
    rp = topi.shape[0]
    tm = _largest_divisor(rp, (256, 128))
    kern = functools.partial(_rank_kernel, tm=tm, lp=lp)
    return pl.pallas_call(
        kern,
        grid=(rp // tm,),
        in_specs=[pl.BlockSpec((tm, LANES), lambda i: (i, 0))],
        out_specs=[pl.BlockSpec((tm, LANES), lambda i: (i, 0)), pl.BlockSpec((1, LANES), lambda i: (0, 0))],
        out_shape=[jax.ShapeDtypeStruct((rp, LANES), jnp.int32), jax.ShapeDtypeStruct((1, LANES), F32)],
        scratch_shapes=[pltpu.VMEM((1, LANES), F32)],
        compiler_params=pltpu.CompilerParams(dimension_semantics=("arbitrary",)),
        name="moe_rank",
    )(topi)


def _row_gather_start(src_hbm, dst, sem, tok, r):
    pltpu.make_async_copy(
        src_hbm.at[pl.ds(pl.multiple_of(tok * ROW_TILE, ROW_TILE), ROW_TILE), :],
        dst.at[pl.ds(pl.multiple_of(r * ROW_TILE, ROW_TILE), ROW_TILE), :],
        sem).start()


def _ffn_kernel(bexp_ref, nact_ref, tokc_ref, tokn_ref, u2_hbm, wgu_hbm, wdn_hbm, bgu_ref, bdn_ref, ys_ref,
                gbuf, ring, gu_sc, acc_sc, gsem, wsem, *, rb, d, dff, tk):
    i = pl.program_id(0)
    nact = nact_ref[0]
    slot = lax.rem(i, 2)
    nka = d // tk
    nkb = dff // tk
    n_chunks = 2 * nka + nkb
    n_ring = 3
    assert n_chunks % n_ring == 0 and dff == d

    def chunk_copy(blk, c, ring_slot):
        e = bexp_ref[blk]
        if c < 2 * nka:
            kt, half = divmod(c, 2)
            src = wgu_hbm.at[e, pl.ds(kt * tk, tk), pl.ds(half * dff, dff)]
        else:
            src = wdn_hbm.at[e, pl.ds((c - 2 * nka) * tk, tk), :]
        return pltpu.make_async_copy(src, ring.at[ring_slot], wsem.at[ring_slot])

    def issue_rows(tok_ref, sl):
        def body(r, carry):
            _row_gather_start(u2_hbm, gbuf.at[sl], gsem.at[sl], tok_ref[0, 0, r], r)
            return carry
        lax.fori_loop(0, rb, body, 0, unroll=8)

    @pl.when(i == 0)
    def _():
        issue_rows(tokc_ref, 0)
        chunk_copy(0, 0, 0).start()
        chunk_copy(0, 1, 1).start()

    @pl.when(i + 1 < nact)
    def _():
        issue_rows(tokn_ref, 1 - slot)

    @pl.when(i < nact)
    def _():
        pltpu.make_async_copy(gbuf.at[slot], gbuf.at[slot], gsem.at[slot]).wait()
        gu_sc[...] = jnp.broadcast_to(bgu_ref[0], gu_sc.shape)
        acc_sc[...] = jnp.broadcast_to(bdn_ref[0], acc_sc.shape)
        xk = None
        for c in range(n_chunks):
            rs = c % n_ring
            chunk_copy(i, c, rs).wait()
            nxt = c + n_ring - 1
            if nxt < n_chunks:
                chunk_copy(i, nxt, nxt % n_ring).start()
            else:
                chunk_copy(i + 1, nxt - n_chunks, nxt % n_ring).start()
            w = ring[rs].astype(BF16)
            if c < 2 * nka:
                kt, half = divmod(c, 2)
                if half == 0:
                    per = tk // LANES
                    xk = jnp.concatenate(
                        [gbuf[slot, pl.ds(kt * per + s, rb, stride=ROW_TILE), :].astype(BF16)
                         for s in range(per)], axis=1)
                cols = slice(half * dff, (half + 1) * dff)
                gu_sc[:, cols] += jnp.dot(xk, w, preferred_element_type=F32)
            else:
                kt = c - 2 * nka
                xg = jnp.minimum(gu_sc[:, kt * tk:(kt + 1) * tk], SWIGLU_LIMIT)
                xl = jnp.clip(gu_sc[:, dff + kt * tk:dff + (kt + 1) * tk], -SWIGLU_LIMIT, SWIGLU_LIMIT)
                act = (xg * _sigmoid(SWIGLU_ALPHA * xg) * (xl + 1.0)).astype(BF16)
                acc_sc[...] += jnp.dot(act, w, preferred_element_type=F32)
        for s in range(ROW_TILE):
            ys_ref[pl.ds(s, rb, stride=ROW_TILE), :] = acc_sc[:, s * LANES:(s + 1) * LANES]

    @pl.when(i == nact)
    def _():
        chunk_copy(i, 0, 0).wait()
        chunk_copy(i, 1, 1).wait()

    @pl.when(i >= nact)
    def _():
        ys_ref[...] = jnp.zeros(ys_ref.shape, F32)


def _moe_ffn(block_exp, nact, slot_tok, u2_rows, w_gu, b_gu, w_dn, b_dn, rb):
    n_e, d, f2 = w_gu.shape
    dff = f2 // 2
    nblk = slot_tok.shape[0]
    tk = 512
    kern = functools.partial(_ffn_kernel, rb=rb, d=d, dff=dff, tk=tk)
    grid_spec = pltpu.PrefetchScalarGridSpec(
        num_scalar_prefetch=2,
        grid=(nblk,),
        in_specs=[
            pl.BlockSpec((1, 1, rb), lambda i, be, na: (i, 0, 0), memory_space=pltpu.SMEM),
            pl.BlockSpec((1, 1, rb), lambda i, be, na: (jnp.minimum(i + 1, nblk - 1), 0, 0),
                         memory_space=pltpu.SMEM),
            pl.BlockSpec(memory_space=pl.ANY),
            pl.BlockSpec(memory_space=pl.ANY),
            pl.BlockSpec(memory_space=pl.ANY),
            pl.BlockSpec((1, 1, f2), lambda i, be, na: (be[i], 0, 0)),
            pl.BlockSpec((1, 1, d), lambda i, be, na: (be[i], 0, 0)),
        ],
        out_specs=pl.BlockSpec((rb * ROW_TILE, LANES), lambda i, be, na: (i, 0)),
        scratch_shapes=[
            pltpu.VMEM((2, rb * ROW_TILE, LANES), F32),
            pltpu.VMEM((3, tk, d), F32),
            pltpu.VMEM((rb, f2), F32),
            pltpu.VMEM((rb, d), F32),
            pltpu.SemaphoreType.DMA((2,)),
            pltpu.SemaphoreType.DMA((3,)),
        ],
    )
    return pl.pallas_call(
        kern,
        grid_spec=grid_spec,
        out_shape=jax.ShapeDtypeStruct((nblk * rb * ROW_TILE, LANES), F32),
        compiler_params=pltpu.CompilerParams(
            dimension_semantics=("arbitrary",), vmem_limit_bytes=VMEM_LIMIT),
        name="moe_ffn",
    )(block_exp, nact, slot_tok, slot_tok, u2_rows, w_gu, w_dn, b_gu.reshape(n_e, 1, f2),
      b_dn.reshape(n_e, 1, d))


def _combine_kernel(dc_ref, dn_ref, ys_hbm, gt_ref, h1_ref, o_ref, cbuf, sem, *, tm, n_steps, n_i):
    n = pl.program_id(0) * n_i + pl.program_id(1)
    slot = lax.rem(n, 2)

    def issue(dref, sl):
        def body(r, carry):
            for kk in range(TOP_K):
                _row_gather_start(ys_hbm, cbuf.at[sl, kk], sem.at[sl], dref[0, 0, r * TOP_K + kk], r)
            return carry
        lax.fori_loop(0, tm, body, 0, unroll=4)

    @pl.when(n == 0)
    def _():
        issue(dc_ref, 0)

    @pl.when(n + 1 < n_steps)
    def _():
        issue(dn_ref, 1 - slot)

    pltpu.make_async_copy(cbuf.at[slot], cbuf.at[slot], sem.at[slot]).wait()
    gates = [jnp.broadcast_to(gt_ref[:, kk:kk + 1], (tm, LANES)) for kk in range(TOP_K)]
    for s in range(ROW_TILE):
        sl = slice(s * LANES, (s + 1) * LANES)
        y = h1_ref[:, sl]
        for kk in range(TOP_K):
            y = y + gates[kk] * cbuf[slot, kk, pl.ds(s, tm, stride=ROW_TILE), :]
        o_ref[0, :, sl] = y


def _combine(dest_x, ys_rows, gate, h1, batch, seq, lp):
    rp, d = h1.shape
    tm = BLOCK
    n_i = seq // tm
    n_steps = batch * n_i
    nlp = lp // tm
    kern = functools.partial(_combine_kernel, tm=tm, n_steps=n_steps, n_i=n_i)
    return pl.pallas_call(
        kern,
        grid=(batch, n_i),
        in_specs=[
            pl.BlockSpec((1, 1, tm * TOP_K), lambda b, i: (b * n_i + i, 0, 0), memory_space=pltpu.SMEM),
            pl.BlockSpec((1, 1, tm * TOP_K), lambda b, i: (jnp.minimum(b * n_i + i + 1, n_steps - 1), 0, 0),
                         memory_space=pltpu.SMEM),
            pl.BlockSpec(memory_space=pl.ANY),
            pl.BlockSpec((tm, LANES), lambda b, i: (b * nlp + 1 + i, 0)),
            pl.BlockSpec((tm, d), lambda b, i: (b * nlp + 1 + i, 0)),
        ],
        out_specs=pl.BlockSpec((1, tm, d), lambda b, i: (b, i, 0)),
        out_shape=jax.ShapeDtypeStruct((batch, seq, d), F32),
        scratch_shapes=[
            pltpu.VMEM((2, TOP_K, tm * ROW_TILE, LANES), F32),
            pltpu.SemaphoreType.DMA((2,)),
        ],
        compiler_params=pltpu.CompilerParams(
            dimension_semantics=("arbitrary", "arbitrary"), vmem_limit_bytes=VMEM_LIMIT),
        name="moe_combine",
    )(dest_x, dest_x, ys_rows, gate, h1)


def kernel(x, meta_tokens, rel_bias, lb_logits, norm1, w_in, q_norm, k_norm, diff_lambda, diff_subln,
           hgrn_norm, w_out, norm2, router_w, router_b, w_gate_up, b_gate_up, w_down, b_down):
    batch, seq, d = x.shape
    assert norm1.shape[0] == 1, "single-layer block"
    assert d == ATT_HEADS * ATT_VD == HG_HEADS * HG_DK == ROW_TILE * LANES and seq % BLOCK == 0
    lp = seq + BLOCK
    rp = batch * lp

    first = jnp.concatenate([jnp.zeros((PAD, d), x.dtype), meta_tokens.astype(x.dtype)], axis=0)
    h_pad = jnp.concatenate([jnp.broadcast_to(first[None], (batch, BLOCK, d)), x], axis=1).reshape(rp, d)

    lb = jax.nn.softmax(lb_logits.astype(F32), axis=0)[0]
    lv = diff_lambda[0].astype(F32)
    lam = (jnp.exp(jnp.sum(lv[0] * lv[1])) - jnp.exp(jnp.sum(lv[2] * lv[3])) + LAMBDA_INIT).reshape(1, 1)
    n_grp = d // ATT_HD
    gains = jnp.concatenate([jnp.tile(q_norm[0].astype(F32) * (ATT_HD ** -0.5 * LOG2E), n_grp),
                             jnp.tile(k_norm[0].astype(F32), n_grp),
                             jnp.ones(((N_SEG - 2) * d,), F32)])
    lbs = jnp.concatenate([jnp.zeros((4 * d,), F32), lb, jnp.zeros((4 * d,), F32)])
    colp = jnp.stack([gains, lbs])

    main, g = _in_proj(h_pad, norm1.astype(F32), w_in[0].astype(BF16), colp)

    blk = _largest_divisor(lp, (640, 512, 384, 256, 128))
    pat_d0, pat_d1 = _bias_patterns(rel_bias)
    subln = (diff_subln[0].astype(F32) * (1.0 - LAMBDA_INIT)).reshape(1, ATT_VD)
    o_att = _diff_attention(main, lam, pat_d0, pat_d1, subln, batch, lp, blk)

    o_hg = _hgrn2(main, g, hgrn_norm.astype(F32).reshape(1, HG_DK), batch, lp)

    rw_pad = jnp.pad(router_w[0].astype(F32), ((0, 0), (0, LANES - N_EXPERTS)))
    rb_pad = jnp.pad(router_b[0].astype(F32), (0, LANES - N_EXPERTS), constant_values=NEG_INF).reshape(1, LANES)
    h1, u2_rows, topi, gate = _out_proj(main, o_att, o_hg, h_pad, w_out[0].astype(BF16), norm2.astype(F32),
                                        rw_pad, rb_pad)

    rank, counts = _moe_rank(topi, lp)

    rb = 512
    n_assign = batch * (seq + N_META) * TOP_K
    nblk = n_assign // rb + N_EXPERTS + 1
    cnt = counts[0, :N_EXPERTS].astype(jnp.int32)
    padded = (cnt + rb - 1) // rb * rb
    pend = jnp.cumsum(padded)
    pstart = pend - padded
    rows = jnp.arange(rp, dtype=jnp.int32)
    valid = (rows % lp) >= PAD
    ti4 = topi[:, :TOP_K]
    dest = pstart[ti4] + rank[:, :TOP_K]
    dest_s = jnp.where(valid[:, None], dest, nblk * rb)
    slot_tok = jnp.zeros((nblk * rb,), jnp.int32).at[dest_s.reshape(-1)].set(
        jnp.repeat(rows, TOP_K), mode="drop")
    nact = (pend[-1] // rb).astype(jnp.int32)
    blk_ids = jnp.minimum(jnp.arange(nblk, dtype=jnp.int32), nact - 1)
    block_exp = jnp.minimum(jnp.sum((pend[None, :] <= (blk_ids * rb)[:, None]).astype(jnp.int32), axis=1),
                            N_EXPERTS - 1)

    ys_rows = _moe_ffn(block_exp, nact.reshape(1), slot_tok.reshape(nblk, 1, rb), u2_rows,
                       w_gate_up[0], b_gate_up[0].astype(F32), w_down[0], b_down[0].astype(F32), rb)

    dest_x = dest.reshape(batch, lp, TOP_K)[:, BLOCK:, :].reshape(batch * seq // BLOCK, 1, BLOCK * TOP_K)
    return _combine(dest_x, ys_rows, gate, h1, batch, seq, lp)
```

```python
import functools
import math

import jax
import jax.numpy as jnp
from jax import lax
from jax.experimental import pallas as pl
from jax.experimental.pallas import tpu as pltpu

N_META = 16
BLOCK = 128
PAD = BLOCK - N_META

ATT_HEADS = 8
ATT_HD = 128
ATT_VD = 2 * ATT_HD

HG_HEADS = 16
HG_DK = 128
HG_CHUNK = 64
HG_SUB = 16

REL_BUCKETS = 32
REL_MAX_DIST = 128

N_EXPERTS = 32
TOP_K = 4
SWIGLU_LIMIT = 7.0
SWIGLU_ALPHA = 1.702

RMS_EPS = 1e-6
NEG_INF = -1e30
LAMBDA_INIT = 0.8 - 0.6 * math.exp(-0.3 * 0)
LOG2E = 1.0 / math.log(2.0)

N_SEG = 9
LANES = 128
SUBLANES = 8
ROW_TILE = 16
VMEM_LIMIT = 56 * 1024 * 1024

F32 = jnp.float32
BF16 = jnp.bfloat16


def _largest_divisor(n, candidates):
    for c in candidates:
        if n % c == 0:
            return c
    raise ValueError(f"no tile in {candidates} divides {n}")


def _sigmoid(x):
    return 1.0 / (1.0 + jnp.exp(-x))


def _inproj_kernel(x_ref, n1_ref, w_ref, cp_ref, main_ref, g_ref, u_ref, *, tq):
    seg = pl.program_id(1)

    @pl.when(seg == 0)
    def _():
        xf = x_ref[...]
        ms = jnp.mean(xf * xf, axis=-1, keepdims=True)
        u_ref[...] = (xf * lax.rsqrt(ms + RMS_EPS) * n1_ref[...]).astype(BF16)

    def slabs(epilogue):
        for q in range(w_ref.shape[1] // tq):
            cs = slice(q * tq, (q + 1) * tq)
            epilogue(jnp.dot(u_ref[...], w_ref[:, cs], preferred_element_type=F32), cs)

    @pl.when(seg <= 1)
    def _():
        def epilogue(acc, cs):
            for gi in range(tq // LANES):
                sl = slice(cs.start + gi * LANES, cs.start + (gi + 1) * LANES)
                y = acc[:, gi * LANES:(gi + 1) * LANES]
                ms = jnp.mean(y * y, axis=-1, keepdims=True)
                main_ref[0, :, sl] = (y * lax.rsqrt(ms + RMS_EPS) * cp_ref[0:1, sl]).astype(BF16)
        slabs(epilogue)

    @pl.when((seg == 2) | (seg == 5))
    def _():
        def epilogue(acc, cs):
            main_ref[0, :, cs] = acc.astype(BF16)
        slabs(epilogue)

    @pl.when((seg == 3) | (seg == 6))
    def _():
        def epilogue(acc, cs):
            main_ref[0, :, cs] = (acc * _sigmoid(acc)).astype(BF16)
        slabs(epilogue)

    @pl.when(seg == 4)
    def _():
        def epilogue(acc, cs):
            lb = cp_ref[1:2, cs]
            sg = _sigmoid(acc)
            g_ref[:, cs] = jnp.log(lb + (1.0 - lb) * sg) * LOG2E
            main_ref[0, :, cs] = ((1.0 - lb) * (1.0 - sg)).astype(BF16)
        slabs(epilogue)

    @pl.when(seg >= 7)
    def _():
        def epilogue(acc, cs):
            main_ref[0, :, cs] = _sigmoid(acc).astype(BF16)
        slabs(epilogue)


def _in_proj(h_pad, n1, w_in_bf, colp):
    rp, d = h_pad.shape
    tm = _largest_divisor(rp, (512, 256, 128))
    kern = functools.partial(_inproj_kernel, tq=512)
    return pl.pallas_call(
        kern,
        grid=(rp // tm, N_SEG),
        in_specs=[
            pl.BlockSpec((tm, d), lambda i, j: (i, 0)),
            pl.BlockSpec((1, d), lambda i, j: (0, 0)),
            pl.BlockSpec((d, d), lambda i, j: (0, j)),
            pl.BlockSpec((2, d), lambda i, j: (0, j)),
        ],
        out_specs=[
            pl.BlockSpec((1, tm, d), lambda i, j: (j, i, 0)),
            pl.BlockSpec((tm, d), lambda i, j: (i, 0)),
        ],
        out_shape=[
            jax.ShapeDtypeStruct((N_SEG, rp, d), BF16),
            jax.ShapeDtypeStruct((rp, d), F32),
        ],
        scratch_shapes=[pltpu.VMEM((tm, d), BF16)],
        compiler_params=pltpu.CompilerParams(
            dimension_semantics=("arbitrary", "arbitrary"), vmem_limit_bytes=VMEM_LIMIT),
        name="in_proj",
    )(h_pad, n1, w_in_bf, colp)


def _t5_bucket(rel):
    n = jnp.maximum(rel, 0)
    max_exact = REL_BUCKETS // 2
    nf = jnp.maximum(n, 1).astype(F32)
    large = max_exact + (jnp.log(nf / max_exact) / math.log(REL_MAX_DIST / max_exact)
                         * (REL_BUCKETS - max_exact)).astype(jnp.int32)
    large = jnp.minimum(large, REL_BUCKETS - 1)
    return jnp.where(n < max_exact, n, large)


def _bias_patterns(rel_bias):
    rb =(rel_bias.astype(F32) - rel_bias.astype(F32)[REL_BUCKETS - 1][None, :]) * LOG2E
    by_dist = rb[_t5_bucket(jnp.arange(2 * BLOCK, dtype=jnp.int32))].T
    r = jnp.arange(BLOCK, dtype=jnp.int32)[:, None]
    c = jnp.arange(BLOCK, dtype=jnp.int32)[None, :]
    d0 = jnp.where((r >= c)[None], by_dist[:, jnp.maximum(r - c, 0)], NEG_INF)
    d1 = by_dist[:, BLOCK + r - c]
    return d0, d1


def _tile_bias(d0, d1, nb, diagonal):
    zero = jnp.zeros((BLOCK, BLOCK), F32)
    neg = jnp.full((BLOCK, BLOCK), NEG_INF, F32)
    rows = []
    for a in range(nb):
        if diagonal:
            blocks = [d0 if c == a else d1 if c == a - 1 else neg if c > a else zero for c in range(nb)]
        else:
            blocks = [d1 if (a == 0 and c == nb - 1) else zero for c in range(nb)]
        rows.append(jnp.concatenate(blocks, axis=1))
    return jnp.concatenate(rows, axis=0)


def _attn_kernel(tb, thg, tqi, tki, lam_ref, q_ref, k_ref, v_ref, d0_ref, d1_ref, sg_ref, o_ref,
                 s_a, s_b, m_sc, l_sc, acc_sc, *, bk, hpb):
    t = pl.program_id(0)
    tp = jnp.maximum(t - 1, 0)
    qi = tqi[t]
    ki = tki[t]
    qic = tqi[tp]
    kic = tki[tp]
    nlb = bk // LANES

    @pl.when(t == 0)
    def _():
        s_b[...] = jnp.zeros(s_b.shape, BF16)

    @pl.when(kic == 0)
    def _():
        m_sc[...] = jnp.full(m_sc.shape, NEG_INF, BF16).astype(F32)
        l_sc[...] = jnp.zeros(l_sc.shape, F32)
        acc_sc[...] = jnp.zeros(acc_sc.shape, F32)

    def scores(hh, m):
        sl = slice(hh * ATT_VD + m * ATT_HD, hh * ATT_VD + (m + 1) * ATT_HD)
        return lax.dot_general(q_ref[0, :, sl], k_ref[0, :, sl], (((1,), (1,)), ((), ())),
                               preferred_element_type=F32)

    def produce(s_ref, bias_of, masked):
        ok = None
        if masked:
            ok = (ki * bk + lax.broadcasted_iota(jnp.int32, (1, bk), 1)) >= PAD
        for hh in range(hpb):
            bias = bias_of(hh)
            for m in range(2):
                s = scores(hh, m)
                if bias is not None:
                    s = s + bias
                if masked:
                    s = jnp.where(ok, s, NEG_INF)
                s_ref[2 * hh + m] = s.astype(BF16)

    def update(hh, m, sb):
        c_idx = 2 * hh + m
        lane_blocks = [sb[:, c * LANES:(c + 1) * LANES] for c in range(nlb)]
        m_old = m_sc[c_idx]
        blk_max = functools.reduce(jnp.maximum, lane_blocks).astype(F32)
        m_new = jnp.maximum(m_old, jnp.max(blk_max, axis=-1, keepdims=True))
        alpha = jnp.exp2(m_old - m_new)
        m_b = m_new.astype(BF16)
        pbs = [jnp.exp2(lb - m_b) for lb in lane_blocks]
        l_sc[c_idx] = alpha * l_sc[c_idx] + functools.reduce(jnp.add, pbs).astype(F32)
        acc_sc[c_idx] = jnp.concatenate([alpha] * (ATT_VD // LANES), axis=1) * acc_sc[c_idx] + jnp.dot(
            jnp.concatenate(pbs, axis=1), v_ref[0, :, hh * ATT_VD:(hh + 1) * ATT_VD],
            preferred_element_type=F32)
        m_sc[c_idx] = m_new

    def consume(s_ref):
        for hh in range(hpb):
            for m in range(2):
                update(hh, m, s_ref[2 * hh + m])

    nb128 = bk // BLOCK
    kinds = [
        (ki == qi, lambda hh: _tile_bias(d0_ref[hh], d1_ref[hh], nb128, True), True),
        (ki == qi - 1, lambda hh: _tile_bias(d0_ref[hh], d1_ref[hh], nb128, False), True),
        ((ki < qi - 1) & (ki == 0), lambda hh: None, True),
        ((ki < qi - 1) & (ki > 0), lambda hh: None, False),
    ]
    even = lax.rem(t, 2) == 0
    for cond, bias_of, masked in kinds:
        for parity, (s_w, s_r) in ((even, (s_a, s_b)), (jnp.logical_not(even), (s_b, s_a))):
            @pl.when(cond & parity)
            def _(s_w=s_w, s_r=s_r, bias_of=bias_of, masked=masked):
                produce(s_w, bias_of, masked)
                consume(s_r)

    @pl.when((t >= 1) & (kic == qic))
    def _():
        lam = lam_ref[0, 0]
        for hh in range(hpb):
            l0 = jnp.sum(l_sc[2 * hh], axis=-1, keepdims=True)
            l1 = jnp.sum(l_sc[2 * hh + 1], axis=-1, keepdims=True)
            o = acc_sc[2 * hh] / l0 - lam * (acc_sc[2 * hh + 1] / l1)
            ms = jnp.mean(o * o, axis=-1, keepdims=True)
            o_ref[:, hh * ATT_VD:(hh + 1) * ATT_VD] = (o * lax.rsqrt(ms + RMS_EPS) * sg_ref[...]).astype(BF16)


def _diff_attention(main, lam, pat_d0, pat_d1, subln, batch, lp, blk):
    _, rp, d = main.shape
    nb = lp // blk
    hpb = 2
    wb = hpb * ATT_VD
    n_hg = ATT_HEADS // hpb
    steps = [(b, hg, qi, ki) for b in range(batch) for hg in range(n_hg)
             for qi in range(nb) for ki in range(qi + 1)]
    steps.append(steps[-1])
    tabs = [jnp.asarray([st[c] for st in steps], jnp.int32) for c in range(4)]
    n_steps = len(steps)
    kern = functools.partial(_attn_kernel, bk=blk, hpb=hpb)

    def prev(t):
        return jnp.maximum(t - 1, 0)

    grid_spec = pltpu.PrefetchScalarGridSpec(
        num_scalar_prefetch=4,
        grid=(n_steps,),
        in_specs=[
            pl.BlockSpec(memory_space=pltpu.SMEM),
            pl.BlockSpec((1, blk, wb), lambda t, tb, thg, tqi, tki: (0, tb[t] * nb + tqi[t], thg[t])),
            pl.BlockSpec((1, blk, wb), lambda t, tb, thg, tqi, tki: (1, tb[t] * nb + tki[t], thg[t])),
            pl.BlockSpec((1, blk, wb),
                         lambda t, tb, thg, tqi, tki: (2, tb[prev(t)] * nb + tki[prev(t)], thg[prev(t)])),
            pl.BlockSpec((hpb, BLOCK, BLOCK), lambda t, tb, thg, tqi, tki: (thg[t], 0, 0)),
            pl.BlockSpec((hpb, BLOCK, BLOCK), lambda t, tb, thg, tqi, tki: (thg[t], 0, 0)),
            pl.BlockSpec((1, ATT_VD), lambda t, tb, thg, tqi, tki: (0, 0)),
        ],
        out_specs=pl.BlockSpec(
            (blk, wb), lambda t, tb, thg, tqi, tki: (tb[prev(t)] * nb + tqi[prev(t)], thg[prev(t)])),
        scratch_shapes=[
            pltpu.VMEM((2 * hpb, blk, blk), BF16),
            pltpu.VMEM((2 * hpb, blk, blk), BF16),
            pltpu.VMEM((2 * hpb, blk, LANES), F32),
            pltpu.VMEM((2 * hpb, blk, LANES), F32),
            pltpu.VMEM((2 * hpb, blk, ATT_VD), F32),
        ],
    )
    return pl.pallas_call(
        kern,
        grid_spec=grid_spec,
        out_shape=jax.ShapeDtypeStruct((rp, d), BF16),
        compiler_params=pltpu.CompilerParams(
            dimension_semantics=("arbitrary",), vmem_limit_bytes=VMEM_LIMIT),
        name="diff_attn",
    )(*tabs, lam, main, main, main, pat_d0, pat_d1, subln)


def _hgrn_kernel(q_ref, k_ref, v_ref, g_ref, gn_ref, o_ref, st_sc, b_sc, kf_sc, vf_sc, *, tc, hb):
    c_len = HG_CHUNK
    n_sub = HG_CHUNK // HG_SUB
    half = SUBLANES

    @pl.when(pl.program_id(2) == 0)
    def _():
        st_sc[...] = jnp.zeros(st_sc.shape, F32)

    ri = lax.broadcasted_iota(jnp.int32, (c_len, c_len), 0)
    ci = lax.broadcasted_iota(jnp.int32, (c_len, c_len), 1)
    tri = (ri >= ci).astype(F32)
    sub_r = ri // HG_SUB
    sub_c = ci // HG_SUB
    r2 = lax.broadcasted_iota(jnp.int32, (2 * LANES, 2 * LANES), 0) // LANES
    c2 = lax.broadcasted_iota(jnp.int32, (2 * LANES, 2 * LANES), 1) // LANES
    ones_bd = (r2 == c2).astype(BF16)
    row8 = lax.broadcasted_iota(jnp.int32, (half, LANES), 0)
    nt = (((1,), (1,)), ((), ()))

    def head(hh, r0, b_all):
        hs = slice(hh * HG_DK, (hh + 1) * HG_DK)
        q = q_ref[0, pl.ds(r0, c_len), hs].astype(F32)
        k = kf_sc[:, hs]
        v_bf = v_ref[0, pl.ds(r0, c_len), hs]
        b = b_all[:, hs]
        b_end = b[c_len - 1:c_len, :]
        st = st_sc[hh]

        o = lax.dot_general((q * jnp.exp2(b)).astype(BF16), st.astype(BF16), nt, preferred_element_type=F32)

        e_rows = [b[(j + 1) * HG_SUB - 1:(j + 1) * HG_SUB, :] for j in range(n_sub)]
        e_full = jnp.concatenate([jnp.broadcast_to(e, (HG_SUB, LANES)) for e in e_rows], axis=0)
        k_rel = (k * jnp.exp2(e_full - b)).astype(BF16)
        q_rel = jnp.concatenate(
            [(q * jnp.exp2(jnp.minimum(b - e_rows[j], 0.0))).astype(BF16) for j in range(n_sub - 1)], axis=0)
        a_all = lax.dot_general(q_rel, k_rel, nt, preferred_element_type=F32)
        a_off = jnp.zeros((c_len, c_len), F32)
        for j in range(n_sub - 1):
            a_off = jnp.where(sub_c == j, a_all[j * c_len:(j + 1) * c_len], a_off)
        a_off = jnp.where(sub_r > sub_c, a_off, 0.0)
        o = o + jnp.dot(a_off.astype(BF16), v_bf, preferred_element_type=F32)

        pieces = []
        meta = []
        for i in range(n_sub):
            for s in range(HG_SUB):
                row = i * HG_SUB + s
                ks = kf_sc[pl.ds(row, 1), hs]
                bs = b_sc[pl.ds(row, 1), hs]
                for hf in range(HG_SUB // half):
                    t0 = hf * half
                    if t0 + half - 1 < s:
                        continue
                    rows = slice(i * HG_SUB + t0, i * HG_SUB + t0 + half)
                    w = q[rows] * ks * jnp.exp2(jnp.minimum(b[rows] - bs, 0.0))
                    if s > t0:
                        w = jnp.where(row8 + t0 >= s, w, 0.0)
                    pieces.append(w.astype(BF16))
                    meta.append((i, s, hf))
        n_pairs = len(pieces) // 2
        lhs = jnp.concatenate(
            [jnp.concatenate([pieces[2 * n], pieces[2 * n + 1]], axis=1) for n in range(n_pairs)], axis=0)
        sums = jnp.dot(lhs, ones_bd, preferred_element_type=F32)
        diag = [[jnp.zeros((half, LANES), F32) for _ in range(HG_SUB // half)] for _ in range(n_sub)]
        for n, (i, s, hf) in enumerate(meta):
            blk = sums[(n // 2) * half:(n // 2 + 1) * half, (n % 2) * LANES:(n % 2 + 1) * LANES]
            vs = vf_sc[pl.ds(i * HG_SUB + s, 1), hs]
            diag[i][hf] = diag[i][hf] + blk * vs
        o = o + jnp.concatenate([d for row in diag for d in row], axis=0)

        k_out = (k * jnp.exp2(b_end - b)).astype(BF16)
        st_sc[hh] = st * jnp.exp2(b_end) + lax.dot_general(
            v_bf, k_out, (((0,), (0,)), ((), ())), preferred_element_type=F32)

        ms = jnp.mean(o * o, axis=-1, keepdims=True)
        o_ref[pl.ds(r0, c_len), hs] = (o * lax.rsqrt(ms + RMS_EPS) * gn_ref[...]).astype(BF16)

    def chunk(c, carry):
        r0 = pl.multiple_of(c * c_len, c_len)
        b_all = jnp.dot(tri, g_ref[pl.ds(r0, c_len), :], precision=lax.Precision.HIGHEST,
                        preferred_element_type=F32)
        b_sc[...] = b_all
        kf_sc[...] = k_ref[0, pl.ds(r0, c_len), :].astype(F32)
        vf_sc[...] = v_ref[0, pl.ds(r0, c_len), :].astype(F32)
        for hh in range(hb):
            head(hh, r0, b_all)
        return carry

    lax.fori_loop(0, tc // c_len, chunk, 0)


def _hgrn2(main, g, gn, batch, lp):
    _, rp, d = main.shape
    tc = _largest_divisor(lp, (640, 512, 256, 128, 64))
    nt_ = lp // tc
    hb = 8
    wb = hb * HG_DK
    kern = functools.partial(_hgrn_kernel, tc=tc, hb=hb)

    def seg_spec(seg):
        return pl.BlockSpec((1, tc, wb), lambda b, h, t: (seg, b * nt_ + t, h))

    return pl.pallas_call(
        kern,
        grid=(batch, HG_HEADS // hb, nt_),
        in_specs=[
            seg_spec(3), seg_spec(4), seg_spec(5),
            pl.BlockSpec((tc, wb), lambda b, h, t: (b * nt_ + t, h)),
            pl.BlockSpec((1, HG_DK), lambda b, h, t: (0, 0)),
        ],
        out_specs=pl.BlockSpec((tc, wb), lambda b, h, t: (b * nt_ + t, h)),
        out_shape=jax.ShapeDtypeStruct((rp, d), BF16),
        scratch_shapes=[
            pltpu.VMEM((hb, HG_DK, HG_DK), F32),
            pltpu.VMEM((HG_CHUNK, wb), F32),
            pltpu.VMEM((HG_CHUNK, wb), F32),
            pltpu.VMEM((HG_CHUNK, wb), F32),
        ],
        compiler_params=pltpu.CompilerParams(
            dimension_semantics=("arbitrary", "arbitrary", "arbitrary"), vmem_limit_bytes=VMEM_LIMIT),
        name="hgrn2",
    )(main, main, main, g, gn)


def _outproj_kernel(ga_ref, gh_ref, og_ref, oa_ref, oh_ref, h_ref, w_ref, n2_ref, rw_ref, rb_ref,
                    h1_ref, u2_ref, ti_ref, gt_ref):
    tm = h_ref.shape[0]
    y = (ga_ref[0].astype(F32) * oa_ref[...].astype(F32)
         + gh_ref[0].astype(F32) * (oh_ref[...].astype(F32) * og_ref[0].astype(F32)))
    h1 = h_ref[...] + jnp.dot(y.astype(BF16), w_ref[...], preferred_element_type=F32)
    h1_ref[...] = h1
    ms = jnp.mean(h1 * h1, axis=-1, keepdims=True)
    u2 = h1 * lax.rsqrt(ms + RMS_EPS) * n2_ref[...]
    for s in range(ROW_TILE):
        u2_ref[pl.ds(s, tm, stride=ROW_TILE), :] = u2[:, s * LANES:(s + 1) * LANES]
    logits = jnp.dot(u2, rw_ref[...], precision=lax.Precision.HIGHEST,
                     preferred_element_type=F32) + rb_ref[...]
    lane = lax.broadcasted_iota(jnp.int32, logits.shape, 1)
    cur = logits
    vals, idxs = [], []
    for _ in range(TOP_K):
        mx = jnp.max(cur, axis=-1, keepdims=True)
        ix = jnp.min(jnp.where(cur == mx, lane, LANES), axis=-1, keepdims=True)
        vals.append(mx)
        idxs.append(ix)
        cur = jnp.where(lane == ix, -jnp.inf, cur)
    es = [jnp.exp(v - vals[0]) for v in vals]
    inv = 1.0 / (es[0] + es[1] + es[2] + es[3])
    ti = jnp.zeros(logits.shape, jnp.int32)
    gt = jnp.zeros(logits.shape, F32)
    for kk in range(TOP_K):
        ti = jnp.where(lane == kk, idxs[kk], ti)
        gt = jnp.where(lane == kk, es[kk] * inv, gt)
    ti_ref[...] = ti
    gt_ref[...] = gt


def _out_proj(main, o_att, o_hg, h_pad, w_out_bf, n2, rw_pad, rb_pad):
    rp, d = h_pad.shape
    tm = _largest_divisor(rp, (256, 128))

    def seg_spec(seg):
        return pl.BlockSpec((1, tm, d), lambda i: (seg, i, 0))

    row = pl.BlockSpec((tm, d), lambda i: (i, 0))
    small = pl.BlockSpec((tm, LANES), lambda i: (i, 0))
    return pl.pallas_call(
        _outproj_kernel,
        grid=(rp // tm,),
        in_specs=[
            seg_spec(7), seg_spec(8), seg_spec(6), row, row, row,
            pl.BlockSpec((d, d), lambda i: (0, 0)),
            pl.BlockSpec((1, d), lambda i: (0, 0)),
            pl.BlockSpec((d, LANES), lambda i: (0, 0)),
            pl.BlockSpec((1, LANES), lambda i: (0, 0)),
        ],
        out_specs=[row, pl.BlockSpec((tm * ROW_TILE, LANES), lambda i: (i, 0)), small, small],
        out_shape=[
            jax.ShapeDtypeStruct((rp, d), F32),
            jax.ShapeDtypeStruct((rp * ROW_TILE, LANES), F32),
            jax.ShapeDtypeStruct((rp, LANES), jnp.int32),
            jax.ShapeDtypeStruct((rp, LANES), F32),
        ],
        compiler_params=pltpu.CompilerParams(
            dimension_semantics=("arbitrary",), vmem_limit_bytes=VMEM_LIMIT),
        name="out_proj",
    )(main, main, main, o_att, o_hg, h_pad, w_out_bf, n2, rw_pad, rb_pad)


def _rank_kernel(ti_ref, rank_ref, cnt_ref, carry_sc, *, tm, lp):
    i = pl.program_id(0)

    @pl.when(i == 0)
    def _():
        carry_sc[...] = jnp.zeros(carry_sc.shape, F32)

    ti = ti_ref[...]
    lane = lax.broadcasted_iota(jnp.int32, (tm, LANES), 1)
    row = i * tm + lax.broadcasted_iota(jnp.int32, (tm, 1), 0)
    valid = lax.rem(row, lp) >= PAD
    hot = [(lane == ti[:, kk:kk + 1]) & valid for kk in range(TOP_K)]
    any_hot = hot[0] | hot[1] | hot[2] | hot[3]
    any_f = jnp.where(any_hot, 1.0, 0.0)
    rr = lax.broadcasted_iota(jnp.int32, (tm, tm), 0)
    cc = lax.broadcasted_iota(jnp.int32, (tm, tm), 1)
    strict = jnp.where(rr > cc, 1.0, 0.0).astype(BF16)
    base = carry_sc[...] + jnp.dot(strict, any_f.astype(BF16), preferred_element_type=F32)
    rank = jnp.zeros((tm, LANES), F32)
    for kk in range(TOP_K):
        rk = jnp.sum(jnp.where(hot[kk], base, 0.0), axis=-1, keepdims=True)
        rank = jnp.where(lane == kk, rk, rank)
    rank_ref[...] = rank.astype(jnp.int32)
    carry_sc[...] = carry_sc[...] + jnp.sum(any_f, axis=0, keepdims=True)
    cnt_ref[...] = carry_sc[...]


def _moe_rank(topi, lp):
    rp = topi.shape[0]
    tm = _largest_divisor(rp, (256, 128))
    kern = functools.partial(_rank_kernel, tm=tm, lp=lp)
    return pl.pallas_call(
        kern,
        grid=(rp // tm,),
        in_specs=[pl.BlockSpec((tm, LANES), lambda i: (i, 0))],
        out_specs=[pl.BlockSpec((tm, LANES), lambda i: (i, 0)), pl.BlockSpec((1, LANES), lambda i: (0, 0))],
        out_shape=[jax.ShapeDtypeStruct((rp, LANES), jnp.int32), jax.ShapeDtypeStruct((1, LANES), F32)],
        scratch_shapes=[pltpu.VMEM((1, LANES), F32)],
        compiler_params=pltpu.CompilerParams(dimension_semantics=("arbitrary",)),
        name="moe_rank",
    )(topi)


def _row_gather_start(src_hbm, dst, sem, tok, r):
    pltpu.make_async_copy(
        src_hbm.at[pl.ds(pl.multiple_of(tok * ROW_TILE, ROW_TILE), ROW_TILE), :],
        dst.at[pl.ds(pl.multiple_of(r * ROW_TILE, ROW_TILE), ROW_TILE), :],
        sem).start()


def _ffn_kernel(bexp_ref, nact_ref, tokc_ref, tokn_ref, u2_hbm, wgu_hbm, wdn_hbm, bgu_ref, bdn_ref, ys_ref,
                gbuf, ring, gu_sc, acc_sc, gsem, wsem, *, rb, d, dff, tk):
    i = pl.program_id(0)
    nact = nact_ref[0]
    slot = lax.rem(i, 2)
    nka = d // tk
    nkb = dff // tk
    n_chunks = 2 * nka + nkb
    n_ring = 3
    assert n_chunks % n_ring == 0 and dff == d

    def chunk_copy(blk, c, ring_slot):
        e = bexp_ref[blk]
        if c < 2 * nka:
            kt, half = divmod(c, 2)
            src = wgu_hbm.at[e, pl.ds(kt * tk, tk), pl.ds(half * dff, dff)]
        else:
            src = wdn_hbm.at[e, pl.ds((c - 2 * nka) * tk, tk), :]
        return pltpu.make_async_copy(src, ring.at[ring_slot], wsem.at[ring_slot])

    def issue_rows(tok_ref, sl):
        def body(r, carry):
            _row_gather_start(u2_hbm, gbuf.at[sl], gsem.at[sl], tok_ref[0, 0, r], r)
            return carry
        lax.fori_loop(0, rb, body, 0, unroll=8)

    @pl.when(i == 0)
    def _():
        issue_rows(tokc_ref, 0)
        chunk_copy(0, 0, 0).start()
        chunk_copy(0, 1, 1).start()

    @pl.when(i + 1 < nact)
    def _():
        issue_rows(tokn_ref, 1 - slot)

    @pl.when(i < nact)
    def _():
        pltpu.make_async_copy(gbuf.at[slot], gbuf.at[slot], gsem.at[slot]).wait()
        gu_sc[...] = jnp.broadcast_to(bgu_ref[0], gu_sc.shape)
        acc_sc[...] = jnp.broadcast_to(bdn_ref[0], acc_sc.shape)
        xk = None
        for c in range(n_chunks):
            rs = c % n_ring
            chunk_copy(i, c, rs).wait()
            nxt = c + n_ring - 1
            if nxt < n_chunks:
                chunk_copy(i, nxt, nxt % n_ring).start()
            else:
                chunk_copy(i + 1, nxt - n_chunks, nxt % n_ring).start()
            w = ring[rs].astype(BF16)
            if c < 2 * nka:
                kt, half = divmod(c, 2)
                if half == 0:
                    per = tk // LANES
                    xk = jnp.concatenate(
                        [gbuf[slot, pl.ds(kt * per + s, rb, stride=ROW_TILE), :].astype(BF16)
                         for s in range(per)], axis=1)
                cols = slice(half * dff, (half + 1) * dff)
                gu_sc[:, cols] += jnp.dot(xk, w, preferred_element_type=F32)
            else:
                kt = c - 2 * nka
                xg = jnp.minimum(gu_sc[:, kt * tk:(kt + 1) * tk], SWIGLU_LIMIT)
                xl = jnp.clip(gu_sc[:, dff + kt * tk:dff + (kt + 1) * tk], -SWIGLU_LIMIT, SWIGLU_LIMIT)
                act = (xg * _sigmoid(SWIGLU_ALPHA * xg) * (xl + 1.0)).astype(BF16)
                acc_sc[...] += jnp.dot(act, w, preferred_element_type=F32)
        for s in range(ROW_TILE):
            ys_ref[pl.ds(s, rb, stride=ROW_TILE), :] = acc_sc[:, s * LANES:(s + 1) * LANES]

    @pl.when(i == nact)
    def _():
        chunk_copy(i, 0, 0).wait()
        chunk_copy(i, 1, 1).wait()

    @pl.when(i >= nact)
    def _():
        ys_ref[...] = jnp.zeros(ys_ref.shape, F32)


def _moe_ffn(block_exp, nact, slot_tok, u2_rows, w_gu, b_gu, w_dn, b_dn, rb):
    n_e, d, f2 = w_gu.shape
    dff = f2 // 2
    nblk = slot_tok.shape[0]
    tk = 512
    kern = functools.partial(_ffn_kernel, rb=rb, d=d, dff=dff, tk=tk)
    grid_spec = pltpu.PrefetchScalarGridSpec(
        num_scalar_prefetch=2,
        grid=(nblk,),
        in_specs=[
            pl.BlockSpec((1, 1, rb), lambda i, be, na: (i, 0, 0), memory_space=pltpu.SMEM),
            pl.BlockSpec((1, 1, rb), lambda i, be, na: (jnp.minimum(i + 1, nblk - 1), 0, 0),
                         memory_space=pltpu.SMEM),
            pl.BlockSpec(memory_space=pl.ANY),
            pl.BlockSpec(memory_space=pl.ANY),
            pl.BlockSpec(memory_space=pl.ANY),
            pl.BlockSpec((1, 1, f2), lambda i, be, na: (be[i], 0, 0)),
            pl.BlockSpec((1, 1, d), lambda i, be, na: (be[i], 0, 0)),
        ],
        out_specs=pl.BlockSpec((rb * ROW_TILE, LANES), lambda i, be, na: (i, 0)),
        scratch_shapes=[
            pltpu.VMEM((2, rb * ROW_TILE, LANES), F32),
            pltpu.VMEM((3, tk, d), F32),
            pltpu.VMEM((rb, f2), F32),
            pltpu.VMEM((rb, d), F32),
            pltpu.SemaphoreType.DMA((2,)),
            pltpu.SemaphoreType.DMA((3,)),
        ],
    )
    return pl.pallas_call(
        kern,
        grid_spec=grid_spec,
        out_shape=jax.ShapeDtypeStruct((nblk * rb * ROW_TILE, LANES), F32),
        compiler_params=pltpu.CompilerParams(
            dimension_semantics=("arbitrary",), vmem_limit_bytes=VMEM_LIMIT),
        name="moe_ffn",
    )(block_exp, nact, slot_tok, slot_tok, u2_rows, w_gu, w_dn, b_gu.reshape(n_e, 1, f2),
      b_dn.reshape(n_e, 1, d))


def _combine_kernel(dc_ref, dn_ref, ys_hbm, gt_ref, h1_ref, o_ref, cbuf, sem, *, tm, n_steps, n_i):
    n = pl.program_id(0) * n_i + pl.program_id(1)
    slot = lax.rem(n, 2)

    def issue(dref, sl):
        def body(r, carry):
            for kk in range(TOP_K):
                _row_gather_start(ys_hbm, cbuf.at[sl, kk], sem.at[sl], dref[0, 0, r * TOP_K + kk], r)
            return carry
        lax.fori_loop(0, tm, body, 0, unroll=4)

    @pl.when(n == 0)
    def _():
        issue(dc_ref, 0)

    @pl.when(n + 1 < n_steps)
    def _():
        issue(dn_ref, 1 - slot)

    pltpu.make_async_copy(cbuf.at[slot], cbuf.at[slot], sem.at[slot]).wait()
    gates = [jnp.broadcast_to(gt_ref[:, kk:kk + 1], (tm, LANES)) for kk in range(TOP_K)]
    for s in range(ROW_TILE):
        sl = slice(s * LANES, (s + 1) * LANES)
        y = h1_ref[:, sl]
        for kk in range(TOP_K):
            y = y + gates[kk] * cbuf[slot, kk, pl.ds(s, tm, stride=ROW_TILE), :]
        o_ref[0, :, sl] = y


def _combine(dest_x, ys_rows, gate, h1, batch, seq, lp):
    rp, d = h1.shape
    tm = BLOCK
    n_i = seq // tm
    n_steps = batch * n_i
    nlp = lp // tm
    kern = functools.partial(_combine_kernel, tm=tm, n_steps=n_steps, n_i=n_i)
    return pl.pallas_call(
        kern,
        grid=(batch, n_i),
        in_specs=[
            pl.BlockSpec((1, 1, tm * TOP_K), lambda b, i: (b * n_i + i, 0, 0), memory_space=pltpu.SMEM),
            pl.BlockSpec((1, 1, tm * TOP_K), lambda b, i: (jnp.minimum(b * n_i + i + 1, n_steps - 1), 0, 0),
                         memory_space=pltpu.SMEM),
            pl.BlockSpec(memory_space=pl.ANY),
            pl.BlockSpec((tm, LANES), lambda b, i: (b * nlp + 1 + i, 0)),
            pl.BlockSpec((tm, d), lambda b, i: (b * nlp + 1 + i, 0)),
        ],
        out_specs=pl.BlockSpec((1, tm, d), lambda b, i: (b, i, 0)),
        out_shape=jax.ShapeDtypeStruct((batch, seq, d), F32),
        scratch_shapes=[
            pltpu.VMEM((2, TOP_K, tm * ROW_TILE, LANES), F32),
            pltpu.SemaphoreType.DMA((2,)),
        ],
        compiler_params=pltpu.CompilerParams(
            dimension_semantics=("arbitrary", "arbitrary"), vmem_limit_bytes=VMEM_LIMIT),
        name="moe_combine",
    )(dest_x, dest_x, ys_rows, gate, h1)


def kernel(x, meta_tokens, rel_bias, lb_logits, norm1, w_in, q_norm, k_norm, diff_lambda, diff_subln,
           hgrn_norm, w_out, norm2, router_w, router_b, w_gate_up, b_gate_up, w_down, b_down):
    batch, seq, d = x.shape
    assert norm1.shape[0] == 1, "single-layer block"
    assert d == ATT_HEADS * ATT_VD == HG_HEADS * HG_DK == ROW_TILE * LANES and seq % BLOCK == 0
    lp = seq + BLOCK
    rp = batch * lp

    first = jnp.concatenate([jnp.zeros((PAD, d), x.dtype), meta_tokens.astype(x.dtype)], axis=0)
    h_pad = jnp.concatenate([jnp.broadcast_to(first[None], (batch, BLOCK, d)), x], axis=1).reshape(rp, d)

    lb = jax.nn.softmax(lb_logits.astype(F32), axis=0)[0]
    lv = diff_lambda[0].astype(F32)
    lam = (jnp.exp(jnp.sum(lv[0] * lv[1])) - jnp.exp(jnp.sum(lv[2] * lv[3])) + LAMBDA_INIT).reshape(1, 1)
    n_grp = d // ATT_HD
    gains = jnp.concatenate([jnp.tile(q_norm[0].astype(F32) * (ATT_HD ** -0.5 * LOG2E), n_grp),
                             jnp.tile(k_norm[0].astype(F32), n_grp),
                             jnp.ones(((N_SEG - 2) * d,), F32)])
    lbs = jnp.concatenate([jnp.zeros((4 * d,), F32), lb, jnp.zeros((4 * d,), F32)])
    colp = jnp.stack([gains, lbs])

    main, g = _in_proj(h_pad, norm1.astype(F32), w_in[0].astype(BF16), colp)

    blk = _largest_divisor(lp, (640, 512, 384, 256, 128))
    pat_d0, pat_d1 = _bias_patterns(rel_bias)
    subln = (diff_subln[0].astype(F32) * (1.0 - LAMBDA_INIT)).reshape(1, ATT_VD)
    o_att = _diff_attention(main, lam, pat_d0, pat_d1, subln, batch, lp, blk)

    o_hg = _hgrn2(main, g, hgrn_norm.astype(F32).reshape(1, HG_DK), batch, lp)

    rw_pad = jnp.pad(router_w[0].astype(F32), ((0, 0), (0, LANES - N_EXPERTS)))
    rb_pad = jnp.pad(router_b[0].astype(F32), (0, LANES - N_EXPERTS), constant_values=NEG_INF).reshape(1, LANES)
    h1, u2_rows, topi, gate = _out_proj(main, o_att, o_hg, h_pad, w_out[0].astype(BF16), norm2.astype(F32),
                                        rw_pad, rb_pad)

    rank, counts = _moe_rank(topi, lp)

    rb = 512
    n_assign = batch * (seq + N_META) * TOP_K
    nblk = n_assign // rb + N_EXPERTS + 1
    cnt = counts[0, :N_EXPERTS].astype(jnp.int32)
    padded = (cnt + rb - 1) // rb * rb
    pend = jnp.cumsum(padded)
    pstart = pend - padded
    rows = jnp.arange(rp, dtype=jnp.int32)
    valid = (rows % lp) >= PAD
    ti4 = topi[:, :TOP_K]
    dest = pstart[ti4] + rank[:, :TOP_K]
    dest_s = jnp.where(valid[:, None], dest, nblk * rb)
    slot_tok = jnp.zeros((nblk * rb,), jnp.int32).at[dest_s.reshape(-1)].set(
        jnp.repeat(rows, TOP_K), mode="drop")
    nact = (pend[-1] // rb).astype(jnp.int32)
    blk_ids = jnp.minimum(jnp.arange(nblk, dtype=jnp.int32), nact - 1)
    block_exp = jnp.minimum(jnp.sum((pend[None, :] <= (blk_ids * rb)[:, None]).astype(jnp.int32), axis=1),
                            N_EXPERTS - 1)

    ys_rows = _moe_ffn(block_exp, nact.reshape(1), slot_tok.reshape(nblk, 1, rb), u2_rows,
                       w_gate_up[0], b_gate_up[0].astype(F32), w_down[0], b_down[0].astype(F32), rb)

    dest_x = dest.reshape(batch, lp, TOP_K)[:, BLOCK:, :].reshape(batch * seq // BLOCK, 1, BLOCK * TOP_K)
    return _combine(dest_x, ys_rows, gate, h1, batch, seq, lp)
```

```python
import functools
import math

import jax
import jax.numpy as jnp
from jax import lax
from jax.experimental import pallas as pl
from jax.experimental.pallas import tpu as pltpu

N_META = 16
BLOCK = 128
PAD = BLOCK - N_META

ATT_HEADS = 8
ATT_HD = 128
ATT_VD = 2 * ATT_HD

HG_HEADS = 16
HG_DK = 128
HG_CHUNK = 64
HG_SUB = 16

REL_BUCKETS = 32
REL_MAX_DIST = 128

N_EXPERTS = 32
TOP_K = 4
SWIGLU_LIMIT = 7.0
SWIGLU_ALPHA = 1.702

RMS_EPS = 1e-6
NEG_INF = -1e30
LAMBDA_INIT = 0.8 - 0.6 * math.exp(-0.3 * 0)
LOG2E = 1.0 / math.log(2.0)

N_SEG = 9
LANES = 128
SUBLANES = 8
ROW_TILE = 16
VMEM_LIMIT = 56 * 1024 * 1024

F32 = jnp.float32
BF16 = jnp.bfloat16


def _largest_divisor(n, candidates):
    for c in candidates:
        if n % c == 0:
            return c
    raise ValueError(f"no tile in {candidates} divides {n}")


def _sigmoid(x):
    return 1.0 / (1.0 + jnp.exp(-x))


def _inproj_kernel(x_ref, n1_ref, w_ref, cp_ref, main_ref, g_ref, u_ref, *, tq):
    seg = pl.program_id(1)

    @pl.when(seg == 0)
    def _():
        xf = x_ref[...]
        ms = jnp.mean(xf * xf, axis=-1, keepdims=True)
        u_ref[...] = (xf * lax.rsqrt(ms + RMS_EPS) * n1_ref[...]).astype(BF16)

    def slabs(epilogue):
        for q in range(w_ref.shape[1] // tq):
            cs = slice(q * tq, (q + 1) * tq)
            epilogue(jnp.dot(u_ref[...], w_ref[:, cs], preferred_element_type=F32), cs)

    @pl.when(seg <= 1)
    def _():
        def epilogue(acc, cs):
            for gi in range(tq // LANES):
                sl = slice(cs.start + gi * LANES, cs.start + (gi + 1) * LANES)
                y = acc[:, gi * LANES:(gi + 1) * LANES]
                ms = jnp.mean(y * y, axis=-1, keepdims=True)
                main_ref[0, :, sl] = (y * lax.rsqrt(ms + RMS_EPS) * cp_ref[0:1, sl]).astype(BF16)
        slabs(epilogue)

    @pl.when((seg == 2) | (seg == 5))
    def _():
        def epilogue(acc, cs):
            main_ref[0, :, cs] = acc.astype(BF16)
        slabs(epilogue)

    @pl.when((seg == 3) | (seg == 6))
    def _():
        def epilogue(acc, cs):
            main_ref[0, :, cs] = (acc * _sigmoid(acc)).astype(BF16)
        slabs(epilogue)

    @pl.when(seg == 4)
    def _():
        def epilogue(acc, cs):
            lb = cp_ref[1:2, cs]
            sg = _sigmoid(acc)
            g_ref[:, cs] = jnp.log(lb + (1.0 - lb) * sg) * LOG2E
            main_ref[0, :, cs] = ((1.0 - lb) * (1.0 - sg)).astype(BF16)
        slabs(epilogue)

    @pl.when(seg >= 7)
    def _():
        def epilogue(acc, cs):
            main_ref[0, :, cs] = _sigmoid(acc).astype(BF16)
        slabs(epilogue)


def _in_proj(h_pad, n1, w_in_bf, colp):
    rp, d = h_pad.shape
    tm = _largest_divisor(rp, (512, 256, 128))
    kern = functools.partial(_inproj_kernel, tq=512)
    return pl.pallas_call(
        kern,
        grid=(rp // tm, N_SEG),
        in_specs=[
            pl.BlockSpec((tm, d), lambda i, j: (i, 0)),
            pl.BlockSpec((1, d), lambda i, j: (0, 0)),
            pl.BlockSpec((d, d), lambda i, j: (0, j)),
            pl.BlockSpec((2, d), lambda i, j: (0, j)),
        ],
        out_specs=[
            pl.BlockSpec((1, tm, d), lambda i, j: (j, i, 0)),
            pl.BlockSpec((tm, d), lambda i, j: (i, 0)),
        ],
        out_shape=[
            jax.ShapeDtypeStruct((N_SEG, rp, d), BF16),
            jax.ShapeDtypeStruct((rp, d), F32),
        ],
        scratch_shapes=[pltpu.VMEM((tm, d), BF16)],
        compiler_params=pltpu.CompilerParams(
            dimension_semantics=("arbitrary", "arbitrary"), vmem_limit_bytes=VMEM_LIMIT),
        name="in_proj",
    )(h_pad, n1, w_in_bf, colp)


def _t5_bucket(rel):
    n = jnp.maximum(rel, 0)
    max_exact = REL_BUCKETS // 2
    nf = jnp.maximum(n, 1).astype(F32)
    large = max_exact + (jnp.log(nf / max_exact) / math.log(REL_MAX_DIST / max_exact)
                         * (REL_BUCKETS - max_exact)).astype(jnp.int32)
    large = jnp.minimum(large, REL_BUCKETS - 1)
    return jnp.where(n < max_exact, n, large)


def _bias_patterns(rel_bias):
    rb =(rel_bias.astype(F32) - rel_bias.astype(F32)[REL_BUCKETS - 1][None, :]) * LOG2E
    by_dist = rb[_t5_bucket(jnp.arange(2 * BLOCK, dtype=jnp.int32))].T
    r = jnp.arange(BLOCK, dtype=jnp.int32)[:, None]
    c = jnp.arange(BLOCK, dtype=jnp.int32)[None, :]
    d0 = jnp.where((r >= c)[None], by_dist[:, jnp.maximum(r - c, 0)], NEG_INF)
    d1 = by_dist[:, BLOCK + r - c]
    return d0, d1


def _tile_bias(d0, d1, nb, diagonal):
    zero = jnp.zeros((BLOCK, BLOCK), F32)
    neg = jnp.full((BLOCK, BLOCK), NEG_INF, F32)
    rows = []
    for a in range(nb):
        if diagonal:
            blocks = [d0 if c == a else d1 if c == a - 1 else neg if c > a else zero for c in range(nb)]
        else:
            blocks = [d1 if (a == 0 and c == nb - 1) else zero for c in range(nb)]
        rows.append(jnp.concatenate(blocks, axis=1))
    return jnp.concatenate(rows, axis=0)


def _attn_kernel(qi_tab, ki_tab, lam_ref, q_ref, k_ref, v_ref, d0_ref, d1_ref, sg_ref, o_ref,
                 m_sc, l_sc, acc_sc, *, bq, bk, hpb):
    p = pl.program_id(2)
    qi = qi_tab[p]
    ki = ki_tab[p]
    nlb = bk // LANES

    @pl.when(ki == 0)
    def _():
        m_sc[...] = jnp.full(m_sc.shape, NEG_INF, BF16).astype(F32)
        l_sc[...] = jnp.zeros(l_sc.shape, F32)
        acc_sc[...] = jnp.zeros(acc_sc.shape, F32)

    def scores(hh, m):
        sl = slice(hh * ATT_VD + m * ATT_HD, hh * ATT_VD + (m + 1) * ATT_HD)
        return lax.dot_general(q_ref[0, :, sl], k_ref[0, :, sl], (((1,), (1,)), ((), ())),
                               preferred_element_type=F32)

    def update(hh, m, s):
        c_idx = 2 * hh + m
        sb = s.astype(BF16)
        lane_blocks = [sb[:, c * LANES:(c + 1) * LANES] for c in range(nlb)]
        m_old = m_sc[c_idx]
        blk_max = functools.reduce(jnp.maximum, lane_blocks).astype(F32)
        m_new = jnp.maximum(m_old, jnp.max(blk_max, axis=-1, keepdims=True))
        alpha = jnp.exp2(m_old - m_new)
        m_b = m_new.astype(BF16)
        pbs = [jnp.exp2(lb - m_b) for lb in lane_blocks]
        l_sc[c_idx] = alpha * l_sc[c_idx] + functools.reduce(jnp.add, pbs).astype(F32)
        acc_sc[c_idx] = jnp.concatenate([alpha] * (ATT_VD // LANES), axis=1) * acc_sc[c_idx] + jnp.dot(
            jnp.concatenate(pbs, axis=1), v_ref[0, :, hh * ATT_VD:(hh + 1) * ATT_VD],
            preferred_element_type=F32)
        m_sc[c_idx] = m_new

    def key_valid():
        col = ki * bk + lax.broadcasted_iota(jnp.int32, (1, bk), 1)
        return col >= PAD

    def sweep(bias_of, masked):
        ok = key_valid() if masked else None
        for hh in range(hpb):
            bias = bias_of(hh)
            for m in range(2):
                s = scores(hh, m)
                if bias is not None:
                    s = s + bias
                if masked:
                    s = jnp.where(ok, s, NEG_INF)
                update(hh, m, s)

    @pl.when(ki == qi)
    def _():
        sweep(lambda hh: _tile_bias(d0_ref[hh], d1_ref[hh], bk // BLOCK, True), True)
        lam = lam_ref[0, 0]
        for hh in range(hpb):
            l0 = jnp.sum(l_sc[2 * hh], axis=-1, keepdims=True)
            l1 = jnp.sum(l_sc[2 * hh + 1], axis=-1, keepdims=True)
            o = acc_sc[2 * hh] / l0 - lam * (acc_sc[2 * hh + 1] / l1)
            ms = jnp.mean(o * o, axis=-1, keepdims=True)
            o_ref[:, hh * ATT_VD:(hh + 1) * ATT_VD] = (o * lax.rsqrt(ms + RMS_EPS) * sg_ref[...]).astype(BF16)

    @pl.when(ki == qi - 1)
    def _():
        sweep(lambda hh: _tile_bias(d0_ref[hh], d1_ref[hh], bk // BLOCK, False), True)

    @pl.when((ki < qi - 1) & (ki == 0))
    def _():
        sweep(lambda hh: None, True)

    @pl.when((ki < qi - 1) & (ki > 0))
    def _():
        sweep(lambda hh: None, False)


def _diff_attention(main, lam, pat_d0, pat_d1, subln, batch, lp, blk):
    _, rp, d = main.shape
    nb = lp // blk
    hpb = 2
    wb = hpb * ATT_VD
    pairs = [(qi, ki) for qi in range(nb) for ki in range(qi + 1)]
    qi_tab = jnp.asarray([a for a, _ in pairs], jnp.int32)
    ki_tab = jnp.asarray([b for _, b in pairs], jnp.int32)
    kern = functools.partial(_attn_kernel, bq=blk, bk=blk, hpb=hpb)
    grid_spec = pltpu.PrefetchScalarGridSpec(
        num_scalar_prefetch=2,
        grid=(batch, ATT_HEADS // hpb, len(pairs)),
        in_specs=[
            pl.BlockSpec(memory_space=pltpu.SMEM),
            pl.BlockSpec((1, blk, wb), lambda b, h, p, qt, kt: (0, b * nb + qt[p], h)),
            pl.BlockSpec((1, blk, wb), lambda b, h, p, qt, kt: (1, b * nb + kt[p], h)),
            pl.BlockSpec((1, blk, wb), lambda b, h, p, qt, kt: (2, b * nb + kt[p], h)),
            pl.BlockSpec((hpb, BLOCK, BLOCK), lambda b, h, p, qt, kt: (h, 0, 0)),
            pl.BlockSpec((hpb, BLOCK, BLOCK), lambda b, h, p, qt, kt: (h, 0, 0)),
            pl.BlockSpec((1, ATT_VD), lambda b, h, p, qt, kt: (0, 0)),
        ],
        out_specs=pl.BlockSpec((blk, wb), lambda b, h, p, qt, kt: (b * nb + qt[p], h)),
        scratch_shapes=[
            pltpu.VMEM((2 * hpb, blk, LANES), F32),
            pltpu.VMEM((2 * hpb, blk, LANES), F32),
            pltpu.VMEM((2 * hpb, blk, ATT_VD), F32),
        ],
    )
    return pl.pallas_call(
        kern,
        grid_spec=grid_spec,
        out_shape=jax.ShapeDtypeStruct((rp, d), BF16),
        compiler_params=pltpu.CompilerParams(
            dimension_semantics=("arbitrary", "arbitrary", "arbitrary"), vmem_limit_bytes=VMEM_LIMIT),
        name="diff_attn",
    )(qi_tab, ki_tab, lam, main, main, main, pat_d0, pat_d1, subln)


def _hgrn_kernel(q_ref, k_ref, v_ref, g_ref, gn_ref, o_ref, st_sc, b_sc, kf_sc, vf_sc, *, tc, hb):
    c_len = HG_CHUNK
    n_sub = HG_CHUNK // HG_SUB
    half = SUBLANES

    @pl.when(pl.program_id(2) == 0)
    def _():
        st_sc[...] = jnp.zeros(st_sc.shape, F32)

    ri = lax.broadcasted_iota(jnp.int32, (c_len, c_len), 0)
    ci = lax.broadcasted_iota(jnp.int32, (c_len, c_len), 1)
    tri = (ri >= ci).astype(F32)
    sub_r = ri // HG_SUB
    sub_c = ci // HG_SUB
    r2 = lax.broadcasted_iota(jnp.int32, (2 * LANES, 2 * LANES), 0) // LANES
    c2 = lax.broadcasted_iota(jnp.int32, (2 * LANES, 2 * LANES), 1) // LANES
    ones_bd = (r2 == c2).astype(BF16)
    row8 = lax.broadcasted_iota(jnp.int32, (half, LANES), 0)
    nt = (((1,), (1,)), ((), ()))

    def head(hh, r0, b_all):
        hs = slice(hh * HG_DK, (hh + 1) * HG_DK)
        q = q_ref[0, pl.ds(r0, c_len), hs].astype(F32)
        k = kf_sc[:, hs]
        v_bf = v_ref[0, pl.ds(r0, c_len), hs]
        b = b_all[:, hs]
        b_end = b[c_len - 1:c_len, :]
        st = st_sc[hh]

        o = lax.dot_general((q * jnp.exp2(b)).astype(BF16), st.astype(BF16), nt, preferred_element_type=F32)

        e_rows = [b[(j + 1) * HG_SUB - 1:(j + 1) * HG_SUB, :] for j in range(n_sub)]
        e_full = jnp.concatenate([jnp.broadcast_to(e, (HG_SUB, LANES)) for e in e_rows], axis=0)
        k_rel = (k * jnp.exp2(e_full - b)).astype(BF16)
        q_rel = jnp.concatenate(
            [(q * jnp.exp2(jnp.minimum(b - e_rows[j], 0.0))).astype(BF16) for j in range(n_sub - 1)], axis=0)
        a_all = lax.dot_general(q_rel, k_rel, nt, preferred_element_type=F32)
        a_off = jnp.zeros((c_len, c_len), F32)
        for j in range(n_sub - 1):
            a_off = jnp.where(sub_c == j, a_all[j * c_len:(j + 1) * c_len], a_off)
        a_off = jnp.where(sub_r > sub_c, a_off, 0.0)
        o = o + jnp.dot(a_off.astype(BF16), v_bf, preferred_element_type=F32)

        pieces = []
        meta = []
        for i in range(n_sub):
            for s in range(HG_SUB):
                row = i * HG_SUB + s
                ks = kf_sc[pl.ds(row, 1), hs]
                bs = b_sc[pl.ds(row, 1), hs]
                for hf in range(HG_SUB // half):
                    t0 = hf * half
                    if t0 + half - 1 < s:
                        continue
                    rows = slice(i * HG_SUB + t0, i * HG_SUB + t0 + half)
                    w = q[rows] * ks * jnp.exp2(jnp.minimum(b[rows] - bs, 0.0))
                    if s > t0:
                        w = jnp.where(row8 + t0 >= s, w, 0.0)
                    pieces.append(w.astype(BF16))
                    meta.append((i, s, hf))
        n_pairs = len(pieces) // 2
        lhs = jnp.concatenate(
            [jnp.concatenate([pieces[2 * n], pieces[2 * n + 1]], axis=1) for n in range(n_pairs)], axis=0)
        sums = jnp.dot(lhs, ones_bd, preferred_element_type=F32)
        diag = [[jnp.zeros((half, LANES), F32) for _ in range(HG_SUB // half)] for _ in range(n_sub)]
        for n, (i, s, hf) in enumerate(meta):
            blk = sums[(n // 2) * half:(n // 2 + 1) * half, (n % 2) * LANES:(n % 2 + 1) * LANES]
            vs = vf_sc[pl.ds(i * HG_SUB + s, 1), hs]
            diag[i][hf] = diag[i][hf] + blk * vs
        o = o + jnp.concatenate([d for row in diag for d in row], axis=0)

        k_out = (k * jnp.exp2(b_end - b)).astype(BF16)
        st_sc[hh] = st * jnp.exp2(b_end) + lax.dot_general(
            v_bf, k_out, (((0,), (0,)), ((), ())), preferred_element_type=F32)

        ms = jnp.mean(o * o, axis=-1, keepdims=True)
        o_ref[pl.ds(r0, c_len), hs] = (o * lax.rsqrt(ms + RMS_EPS) * gn_ref[...]).astype(BF16)

    def chunk(c, carry):
        r0 = pl.multiple_of(c * c_len, c_len)
        b_all = jnp.dot(tri, g_ref[pl.ds(r0, c_len), :], precision=lax.Precision.HIGHEST,
                        preferred_element_type=F32)
        b_sc[...] = b_all
        kf_sc[...] = k_ref[0, pl.ds(r0, c_len), :].astype(F32)
        vf_sc[...] = v_ref[0, pl.ds(r0, c_len), :].astype(F32)
        for hh in range(hb):
            head(hh, r0, b_all)
        return carry

    lax.fori_loop(0, tc // c_len, chunk, 0)


def _hgrn2(main, g, gn, batch, lp):
    _, rp, d = main.shape
    tc = _largest_divisor(lp, (640, 512, 256, 128, 64))
    nt_ = lp // tc
    hb = 8
    wb = hb * HG_DK
    kern = functools.partial(_hgrn_kernel, tc=tc, hb=hb)

    def seg_spec(seg):
        return pl.BlockSpec((1, tc, wb), lambda b, h, t: (seg, b * nt_ + t, h))

    return pl.pallas_call(
        kern,
        grid=(batch, HG_HEADS // hb, nt_),
        in_specs=[
            seg_spec(3), seg_spec(4), seg_spec(5),
            pl.BlockSpec((tc, wb), lambda b, h, t: (b * nt_ + t, h)),
            pl.BlockSpec((1, HG_DK), lambda b, h, t: (0, 0)),
        ],
        out_specs=pl.BlockSpec((tc, wb), lambda b, h, t: (b * nt_ + t, h)),
        out_shape=jax.ShapeDtypeStruct((rp, d), BF16),
        scratch_shapes=[
            pltpu.VMEM((hb, HG_DK, HG_DK), F32),
            pltpu.VMEM((HG_CHUNK, wb), F32),
            pltpu.VMEM((HG_CHUNK, wb), F32),
            pltpu.VMEM((HG_CHUNK, wb), F32),
        ],
        compiler_params=pltpu.CompilerParams(
            dimension_semantics=("arbitrary", "arbitrary", "arbitrary"), vmem_limit_bytes=VMEM_LIMIT),
        name="hgrn2",
    )(main, main, main, g, gn)


def _outproj_kernel(ga_ref, gh_ref, og_ref, oa_ref, oh_ref, h_ref, w_ref, n2_ref, rw_ref, rb_ref,
                    h1_ref, u2_ref, ti_ref, gt_ref):
    tm = h_ref.shape[0]
    y = (ga_ref[0].astype(F32) * oa_ref[...].astype(F32)
         + gh_ref[0].astype(F32) * (oh_ref[...].astype(F32) * og_ref[0].astype(F32)))
    h1 = h_ref[...] + jnp.dot(y.astype(BF16), w_ref[...], preferred_element_type=F32)
    h1_ref[...] = h1
    ms = jnp.mean(h1 * h1, axis=-1, keepdims=True)
    u2 = h1 * lax.rsqrt(ms + RMS_EPS) * n2_ref[...]
    for s in range(ROW_TILE):
        u2_ref[pl.ds(s, tm, stride=ROW_TILE), :] = u2[:, s * LANES:(s + 1) * LANES]
    logits = jnp.dot(u2, rw_ref[...], precision=lax.Precision.HIGHEST,
                     preferred_element_type=F32) + rb_ref[...]
    lane = lax.broadcasted_iota(jnp.int32, logits.shape, 1)
    cur = logits
    vals, idxs = [], []
    for _ in range(TOP_K):
        mx = jnp.max(cur, axis=-1, keepdims=True)
        ix = jnp.min(jnp.where(cur == mx, lane, LANES), axis=-1, keepdims=True)
        vals.append(mx)
        idxs.append(ix)
        cur = jnp.where(lane == ix, -jnp.inf, cur)
    es = [jnp.exp(v - vals[0]) for v in vals]
    inv = 1.0 / (es[0] + es[1] + es[2] + es[3])
    ti = jnp.zeros(logits.shape, jnp.int32)
    gt = jnp.zeros(logits.shape, F32)
    for kk in range(TOP_K):
        ti = jnp.where(lane == kk, idxs[kk], ti)
        gt = jnp.where(lane == kk, es[kk] * inv, gt)
    ti_ref[...] = ti
    gt_ref[...] = gt


def _out_proj(main, o_att, o_hg, h_pad, w_out_bf, n2, rw_pad, rb_pad):
    rp, d = h_pad.shape
    tm = _largest_divisor(rp, (256, 128))

    def seg_spec(seg):
        return pl.BlockSpec((1, tm, d), lambda i: (seg, i, 0))

    row = pl.BlockSpec((tm, d), lambda i: (i, 0))
    small = pl.BlockSpec((tm, LANES), lambda i: (i, 0))
    return pl.pallas_call(
        _outproj_kernel,
        grid=(rp // tm,),
        in_specs=[
            seg_spec(7), seg_spec(8), seg_spec(6), row, row, row,
            pl.BlockSpec((d, d), lambda i: (0, 0)),
            pl.BlockSpec((1, d), lambda i: (0, 0)),
            pl.BlockSpec((d, LANES), lambda i: (0, 0)),
            pl.BlockSpec((1, LANES), lambda i: (0, 0)),
        ],
        out_specs=[row, pl.BlockSpec((tm * ROW_TILE, LANES), lambda i: (i, 0)), small, small],
        out_shape=[
            jax.ShapeDtypeStruct((rp, d), F32),
            jax.ShapeDtypeStruct((rp * ROW_TILE, LANES), F32),
            jax.ShapeDtypeStruct((rp, LANES), jnp.int32),
            jax.ShapeDtypeStruct((rp, LANES), F32),
        ],
        compiler_params=pltpu.CompilerParams(
            dimension_semantics=("arbitrary",), vmem_limit_bytes=VMEM_LIMIT),
        name="out_proj",
    )(main, main, main, o_att, o_hg, h_pad, w_out_bf, n2, rw_pad, rb_pad)


def _rank_kernel(ti_ref, rank_ref, cnt_ref, carry_sc, *, tm, lp):
    i = pl.program_id(0)

    @pl.when(i == 0)
    def _():
        carry_sc[...] = jnp.zeros(carry_sc.shape, F32)

    ti = ti_ref[...]
    lane = lax.broadcasted_iota(jnp.int32, (tm, LANES), 1)
    row = i * tm + lax.broadcasted_iota(jnp.int32, (tm, 1), 0)
    valid = lax.rem(row, lp) >= PAD
    hot = [(lane == ti[:, kk:kk + 1]) & valid for kk in range(TOP_K)]
    any_hot = hot[0] | hot[1] | hot[2] | hot[3]
    any_f = jnp.where(any_hot, 1.0, 0.0)
    rr = lax.broadcasted_iota(jnp.int32, (tm, tm), 0)
    cc = lax.broadcasted_iota(jnp.int32, (tm, tm), 1)
    strict = jnp.where(rr > cc, 1.0, 0.0).astype(BF16)
    base = carry_sc[...] + jnp.dot(strict, any_f.astype(BF16), preferred_element_type=F32)
    rank = jnp.zeros((tm, LANES), F32)
    for kk in range(TOP_K):
        rk = jnp.sum(jnp.where(hot[kk], base, 0.0), axis=-1, keepdims=True)
        rank = jnp.where(lane == kk, rk, rank)
    rank_ref[...] = rank.astype(jnp.int32)
    carry_sc[...] = carry_sc[...] + jnp.sum(any_f, axis=0, keepdims=True)
    cnt_ref[...] = carry_sc[...]


def _moe_rank(topi, lp):
    rp = topi.shape[0]
    tm = _largest_divisor(rp, (256, 128))
    kern = functools.partial(_rank_kernel, tm=tm, lp=lp)
    return pl.pallas_call(
        kern,
        grid=(rp // tm,),
        in_specs=[pl.BlockSpec((tm, LANES), lambda i: (i, 0))],
        out_specs=[pl.BlockSpec((tm, LANES), lambda i: (i, 0)), pl.BlockSpec((1, LANES), lambda i: (0, 0))],
        out_shape=[jax.ShapeDtypeStruct((rp, LANES), jnp.int32), jax.ShapeDtypeStruct((1, LANES), F32)],
        scratch_shapes=[pltpu.VMEM((1, LANES), F32)],
        compiler_params=pltpu.CompilerParams(dimension_semantics=("arbitrary",)),
        name="moe_rank",
    )(topi)


def _row_gather_start(src_hbm, dst, sem, tok, r):
    pltpu.make_async_copy(
        src_hbm.at[pl.ds(pl.multiple_of(tok * ROW_TILE, ROW_TILE), ROW_TILE), :],
        dst.at[pl.ds(pl.multiple_of(r * ROW_TILE, ROW_TILE), ROW_TILE), :],
        sem).start()


def _ffn_kernel(bexp_ref, nact_ref, tokc_ref, tokn_ref, u2_hbm, wgu_hbm, wdn_hbm, bgu_ref, bdn_ref, ys_ref,
                gbuf, ring, gu_sc, acc_sc, gsem, wsem, *, rb, d, dff, tk):
    i = pl.program_id(0)
    nact = nact_ref[0]
    slot = lax.rem(i, 2)
    nka = d // tk
    nkb = dff // tk
    n_chunks = 2 * nka + nkb
    n_ring = ring.shape[0]
    assert n_chunks % n_ring == 0 and dff == d

    def chunk_copy(blk, c, ring_slot):
        e = bexp_ref[blk]
        if c < 2 * nka:
            kt, half = divmod(c, 2)
            src = wgu_hbm.at[e, pl.ds(kt * tk, tk), pl.ds(half * dff, dff)]
        else:
            src = wdn_hbm.at[e, pl.ds((c - 2 * nka) * tk, tk), :]
        return pltpu.make_async_copy(src, ring.at[ring_slot], wsem.at[ring_slot])

    def issue_rows(tok_ref, sl):
        def body(r, carry):
            _row_gather_start(u2_hbm, gbuf.at[sl], gsem.at[sl], tok_ref[0, 0, r], r)
            return carry
        lax.fori_loop(0, rb, body, 0, unroll=8)

    @pl.when(i == 0)
    def _():
        issue_rows(tokc_ref, 0)
        for c in range(n_ring - 1):
            chunk_copy(0, c, c).start()

    @pl.when(i + 1 < nact)
    def _():
        issue_rows(tokn_ref, 1 - slot)

    @pl.when(i < nact)
    def _():
        pltpu.make_async_copy(gbuf.at[slot], gbuf.at[slot], gsem.at[slot]).wait()
        gu_sc[...] = jnp.broadcast_to(bgu_ref[0], gu_sc.shape)
        acc_sc[...] = jnp.broadcast_to(bdn_ref[0], acc_sc.shape)
        xk = None
        for c in range(n_chunks):
            rs = c % n_ring
            chunk_copy(i, c, rs).wait()
            nxt = c + n_ring - 1
            if nxt < n_chunks:
                chunk_copy(i, nxt, nxt % n_ring).start()
            else:
                chunk_copy(i + 1, nxt - n_chunks, nxt % n_ring).start()
            w = ring[rs].astype(BF16)
            if c < 2 * nka:
                kt, half = divmod(c, 2)
                if half == 0:
                    per = tk // LANES
                    xk = jnp.concatenate(
                        [gbuf[slot, pl.ds(kt * per + s, rb, stride=ROW_TILE), :].astype(BF16)
                         for s in range(per)], axis=1)
                cols = slice(half * dff, (half + 1) * dff)
                gu_sc[:, cols] += jnp.dot(xk, w, preferred_element_type=F32)
            else:
                kt = c - 2 * nka
                xg = jnp.minimum(gu_sc[:, kt * tk:(kt + 1) * tk], SWIGLU_LIMIT)
                xl = jnp.clip(gu_sc[:, dff + kt * tk:dff + (kt + 1) * tk], -SWIGLU_LIMIT, SWIGLU_LIMIT)
                act = (xg * _sigmoid(SWIGLU_ALPHA * xg) * (xl + 1.0)).astype(BF16)
                acc_sc[...] += jnp.dot(act, w, preferred_element_type=F32)
        for s in range(ROW_TILE):
            ys_ref[pl.ds(s, rb, stride=ROW_TILE), :] = acc_sc[:, s * LANES:(s + 1) * LANES]

    @pl.when(i == nact)
    def _():
        for c in range(n_ring - 1):
            chunk_copy(i, c, c).wait()

    @pl.when(i >= nact)
    def _():
        ys_ref[...] = jnp.zeros(ys_ref.shape, F32)


def _moe_ffn(block_exp, nact, slot_tok, u2_rows, w_gu, b_gu, w_dn, b_dn, rb):
    n_e, d, f2 = w_gu.shape
    dff = f2 // 2
    nblk = slot_tok.shape[0]
    tk = 512
    n_ring = 4
    kern = functools.partial(_ffn_kernel, rb=rb, d=d, dff=dff, tk=tk)
    grid_spec = pltpu.PrefetchScalarGridSpec(
        num_scalar_prefetch=2,
        grid=(nblk,),
        in_specs=[
            pl.BlockSpec((1, 1, rb), lambda i, be, na: (i, 0, 0), memory_space=pltpu.SMEM),
            pl.BlockSpec((1, 1, rb), lambda i, be, na: (jnp.minimum(i + 1, nblk - 1), 0, 0),
                         memory_space=pltpu.SMEM),
            pl.BlockSpec(memory_space=pl.ANY),
            pl.BlockSpec(memory_space=pl.ANY),
            pl.BlockSpec(memory_space=pl.ANY),
            pl.BlockSpec((1, 1, f2), lambda i, be, na: (be[i], 0, 0)),
            pl.BlockSpec((1, 1, d), lambda i, be, na: (be[i], 0, 0)),
        ],
        out_specs=pl.BlockSpec((rb * ROW_TILE, LANES), lambda i, be, na: (i, 0)),
        scratch_shapes=[
            pltpu.VMEM((2, rb * ROW_TILE, LANES), F32),
            pltpu.VMEM((n_ring, tk, d), F32),
            pltpu.VMEM((rb, f2), F32),
            pltpu.VMEM((rb, d), F32),
            pltpu.SemaphoreType.DMA((2,)),
            pltpu.SemaphoreType.DMA((n_ring,)),
        ],
    )
    return pl.pallas_call(
        kern,
        grid_spec=grid_spec,
        out_shape=jax.ShapeDtypeStruct((nblk * rb * ROW_TILE, LANES), F32),
        compiler_params=pltpu.CompilerParams(
            dimension_semantics=("arbitrary",), vmem_limit_bytes=VMEM_LIMIT),
        name="moe_ffn",
    )(block_exp, nact, slot_tok, slot_tok, u2_rows, w_gu, w_dn, b_gu.reshape(n_e, 1, f2),
      b_dn.reshape(n_e, 1, d))


def _combine_kernel(dc_ref, dn_ref, ys_hbm, gt_ref, h1_ref, o_ref, cbuf, sem, *, tm, n_steps, n_i):
    n = pl.program_id(0) * n_i + pl.program_id(1)
    slot = lax.rem(n, 2)

    def issue(dref, sl):
        def body(r, carry):
            for kk in range(TOP_K):
                _row_gather_start(ys_hbm, cbuf.at[sl, kk], sem.at[sl], dref[0, 0, r * TOP_K + kk], r)
            return carry
        lax.fori_loop(0, tm, body, 0, unroll=4)

    @pl.when(n == 0)
    def _():
        issue(dc_ref, 0)

    @pl.when(n + 1 < n_steps)
    def _():
        issue(dn_ref, 1 - slot)

    pltpu.make_async_copy(cbuf.at[slot], cbuf.at[slot], sem.at[slot]).wait()
    gates = [jnp.broadcast_to(gt_ref[:, kk:kk + 1], (tm, LANES)) for kk in range(TOP_K)]
    for s in range(ROW_TILE):
        sl = slice(s * LANES, (s + 1) * LANES)
        y = h1_ref[:, sl]
        for kk in range(TOP_K):
            y = y + gates[kk] * cbuf[slot, kk, pl.ds(s, tm, stride=ROW_TILE), :]
        o_ref[0, :, sl] = y


def _combine(dest_x, ys_rows, gate, h1, batch, seq, lp):
    rp, d = h1.shape
    tm = BLOCK
    n_i = seq // tm
    n_steps = batch * n_i
    nlp = lp // tm
    kern = functools.partial(_combine_kernel, tm=tm, n_steps=n_steps, n_i=n_i)
    return pl.pallas_call(
        kern,
        grid=(batch, n_i),
        in_specs=[
            pl.BlockSpec((1, 1, tm * TOP_K), lambda b, i: (b * n_i + i, 0, 0), memory_space=pltpu.SMEM),
            pl.BlockSpec((1, 1, tm * TOP_K), lambda b, i: (jnp.minimum(b * n_i + i + 1, n_steps - 1), 0, 0),
                         memory_space=pltpu.SMEM),
            pl.BlockSpec(memory_space=pl.ANY),
            pl.BlockSpec((tm, LANES), lambda b, i: (b * nlp + 1 + i, 0)),
            pl.BlockSpec((tm, d), lambda b, i: (b * nlp + 1 + i, 0)),
        ],
        out_specs=pl.BlockSpec((1, tm, d), lambda b, i: (b, i, 0)),
        out_shape=jax.ShapeDtypeStruct((batch, seq, d), F32),
        scratch_shapes=[
            pltpu.VMEM((2, TOP_K, tm * ROW_TILE, LANES), F32),
            pltpu.SemaphoreType.DMA((2,)),
        ],
        compiler_params=pltpu.CompilerParams(
            dimension_semantics=("arbitrary", "arbitrary"), vmem_limit_bytes=VMEM_LIMIT),
        name="moe_combine",
    )(dest_x, dest_x, ys_rows, gate, h1)


def kernel(x, meta_tokens, rel_bias, lb_logits, norm1, w_in, q_norm, k_norm, diff_lambda, diff_subln,
           hgrn_norm, w_out, norm2, router_w, router_b, w_gate_up, b_gate_up, w_down, b_down):
    batch, seq, d = x.shape
    assert norm1.shape[0] == 1, "single-layer block"
    assert d == ATT_HEADS * ATT_VD == HG_HEADS * HG_DK == ROW_TILE * LANES and seq % BLOCK == 0
    lp = seq + BLOCK
    rp = batch * lp

    first = jnp.concatenate([jnp.zeros((PAD, d), x.dtype), meta_tokens.astype(x.dtype)], axis=0)
    h_pad = jnp.concatenate([jnp.broadcast_to(first[None], (batch, BLOCK, d)), x], axis=1).reshape(rp, d)

    lb = jax.nn.softmax(lb_logits.astype(F32), axis=0)[0]
    lv = diff_lambda[0].astype(F32)
    lam = (jnp.exp(jnp.sum(lv[0] * lv[1])) - jnp.exp(jnp.sum(lv[2] * lv[3])) + LAMBDA_INIT).reshape(1, 1)
    n_grp = d // ATT_HD
    gains = jnp.concatenate([jnp.tile(q_norm[0].astype(F32) * (ATT_HD ** -0.5 * LOG2E), n_grp),
                             jnp.tile(k_norm[0].astype(F32), n_grp),
                             jnp.ones(((N_SEG - 2) * d,), F32)])
    lbs = jnp.concatenate([jnp.zeros((4 * d,), F32), lb, jnp.zeros((4 * d,), F32)])
    colp = jnp.stack([gains, lbs])

    main, g = _in_proj(h_pad, norm1.astype(F32), w_in[0].astype(BF16), colp)

    blk = _largest_divisor(lp, (640, 512, 384, 256, 128))
    pat_d0, pat_d1 = _bias_patterns(rel_bias)
    subln = (diff_subln[0].astype(F32) * (1.0 - LAMBDA_INIT)).reshape(1, ATT_VD)
    o_att = _diff_attention(main, lam, pat_d0, pat_d1, subln, batch, lp, blk)

    o_hg = _hgrn2(main, g, hgrn_norm.astype(F32).reshape(1, HG_DK), batch, lp)

    rw_pad = jnp.pad(router_w[0].astype(F32), ((0, 0), (0, LANES - N_EXPERTS)))
    rb_pad = jnp.pad(router_b[0].astype(F32), (0, LANES - N_EXPERTS), constant_values=NEG_INF).reshape(1, LANES)
    h1, u2_rows, topi, gate = _out_proj(main, o_att, o_hg, h_pad, w_out[0].astype(BF16), norm2.astype(F32),
                                        rw_pad, rb_pad)

    rank, counts = _moe_rank(topi, lp)

    rb = 512
    n_assign = batch * (seq + N_META) * TOP_K
    nblk = n_assign // rb + N_EXPERTS + 1
    cnt = counts[0, :N_EXPERTS].astype(jnp.int32)
    padded = (cnt + rb - 1) // rb * rb
    pend = jnp.cumsum(padded)
    pstart = pend - padded
    rows = jnp.arange(rp, dtype=jnp.int32)
    valid = (rows % lp) >= PAD
    ti4 = topi[:, :TOP_K]
    dest = pstart[ti4] + rank[:, :TOP_K]
    oob = nblk * rb + rows[:, None] * TOP_K + jnp.arange(TOP_K, dtype=jnp.int32)[None, :]
    dest_s = jnp.where(valid[:, None], dest, oob)
    slot_tok = jnp.zeros((nblk * rb,), jnp.int32).at[dest_s.reshape(-1)].set(
        jnp.repeat(rows, TOP_K), mode="drop", unique_indices=True)
    nact = (pend[-1] // rb).astype(jnp.int32)
    blk_ids = jnp.minimum(jnp.arange(nblk, dtype=jnp.int32), nact - 1)
    block_exp = jnp.minimum(jnp.sum((pend[None, :] <= (blk_ids * rb)[:, None]).astype(jnp.int32), axis=1),
                            N_EXPERTS - 1)

    ys_rows = _moe_ffn(block_exp, nact.reshape(1), slot_tok.reshape(nblk, 1, rb), u2_rows,
                       w_gate_up[0], b_gate_up[0].astype(F32), w_down[0], b_down[0].astype(F32), rb)

    dest_x = dest.reshape(batch, lp, TOP_K)[:, BLOCK:, :].reshape(batch * seq // BLOCK, 1, BLOCK * TOP_K)
    return _combine(dest_x, ys_rows, gate, h1, batch, seq, lp)
```

```python
import functools
import math

import jax
import jax.numpy as jnp
from jax import lax
from jax.experimental import pallas as pl
from jax.experimental.pallas import tpu as pltpu

N_META = 16
BLOCK = 128
PAD = BLOCK - N_META

ATT_HEADS = 8
ATT_HD = 128
ATT_VD = 2 * ATT_HD

HG_HEADS = 16
HG_DK = 128
HG_CHUNK = 64
HG_SUB = 16

REL_BUCKETS = 32
REL_MAX_DIST = 128

N_EXPERTS = 32
TOP_K = 4
SWIGLU_LIMIT = 7.0
SWIGLU_ALPHA = 1.702

RMS_EPS = 1e-6
NEG_INF = -1e30
LAMBDA_INIT = 0.8 - 0.6 * math.exp(-0.3 * 0)
LOG2E = 1.0 / math.log(2.0)

N_SEG = 9
LANES = 128
SUBLANES = 8
ROW_TILE = 16
VMEM_LIMIT = 56 * 1024 * 1024

F32 = jnp.float32
BF16 = jnp.bfloat16


def _largest_divisor(n, candidates):
    for c in candidates:
        if n % c == 0:
            return c
    raise ValueError(f"no tile in {candidates} divides {n}")


def _sigmoid(x):
    return 1.0 / (1.0 + jnp.exp(-x))


def _inproj_kernel(x_ref, n1_ref, w_ref, cp_ref, main_ref, g_ref, u_ref, *, tq):
    seg = pl.program_id(1)

    @pl.when(seg == 0)
    def _():
        xf = x_ref[...]
        ms = jnp.mean(xf * xf, axis=-1, keepdims=True)
        u_ref[...] = (xf * lax.rsqrt(ms + RMS_EPS) * n1_ref[...]).astype(BF16)

    def slabs(epilogue):
        for q in range(w_ref.shape[1] // tq):
            cs = slice(q * tq, (q + 1) * tq)
            epilogue(jnp.dot(u_ref[...], w_ref[:, cs], preferred_element_type=F32), cs)

    @pl.when(seg <= 1)
    def _():
        def epilogue(acc, cs):
            for gi in range(tq // LANES):
                sl = slice(cs.start + gi * LANES, cs.start + (gi + 1) * LANES)
                y = acc[:, gi * LANES:(gi + 1) * LANES]
                ms = jnp.mean(y * y, axis=-1, keepdims=True)
                main_ref[0, :, sl] = (y * lax.rsqrt(ms + RMS_EPS) * cp_ref[0:1, sl]).astype(BF16)
        slabs(epilogue)

    @pl.when((seg == 2) | (seg == 5))
    def _():
        def epilogue(acc, cs):
            main_ref[0, :, cs] = acc.astype(BF16)
        slabs(epilogue)

    @pl.when((seg == 3) | (seg == 6))
    def _():
        def epilogue(acc, cs):
            main_ref[0, :, cs] = (acc * _sigmoid(acc)).astype(BF16)
        slabs(epilogue)

    @pl.when(seg == 4)
    def _():
        def epilogue(acc, cs):
            lb = cp_ref[1:2, cs]
            sg = _sigmoid(acc)
            g_ref[:, cs] = jnp.log(lb + (1.0 - lb) * sg) * LOG2E
            main_ref[0, :, cs] = ((1.0 - lb) * (1.0 - sg)).astype(BF16)
        slabs(epilogue)

    @pl.when(seg >= 7)
    def _():
        def epilogue(acc, cs):
            main_ref[0, :, cs] = _sigmoid(acc).astype(BF16)
        slabs(epilogue)


def _in_proj(h_pad, n1, w_in_bf, colp):
    rp, d = h_pad.shape
    tm = _largest_divisor(rp, (512, 256, 128))
    kern = functools.partial(_inproj_kernel, tq=512)
    return pl.pallas_call(
        kern,
        grid=(rp // tm, N_SEG),
        in_specs=[
            pl.BlockSpec((tm, d), lambda i, j: (i, 0)),
            pl.BlockSpec((1, d), lambda i, j: (0, 0)),
            pl.BlockSpec((d, d), lambda i, j: (0, j)),
            pl.BlockSpec((2, d), lambda i, j: (0, j)),
        ],
        out_specs=[
            pl.BlockSpec((1, tm, d), lambda i, j: (j, i, 0)),
            pl.BlockSpec((tm, d), lambda i, j: (i, 0)),
        ],
        out_shape=[
            jax.ShapeDtypeStruct((N_SEG, rp, d), BF16),
            jax.ShapeDtypeStruct((rp, d), F32),
        ],
        scratch_shapes=[pltpu.VMEM((tm, d), BF16)],
        compiler_params=pltpu.CompilerParams(
            dimension_semantics=("arbitrary", "arbitrary"), vmem_limit_bytes=VMEM_LIMIT),
        name="in_proj",
    )(h_pad, n1, w_in_bf, colp)


def _t5_bucket(rel):
    n = jnp.maximum(rel, 0)
    max_exact = REL_BUCKETS // 2
    nf = jnp.maximum(n, 1).astype(F32)
    large = max_exact + (jnp.log(nf / max_exact) / math.log(REL_MAX_DIST / max_exact)
                         * (REL_BUCKETS - max_exact)).astype(jnp.int32)
    large = jnp.minimum(large, REL_BUCKETS - 1)
    return jnp.where(n < max_exact, n, large)


def _bias_patterns(rel_bias):
    rb =(rel_bias.astype(F32) - rel_bias.astype(F32)[REL_BUCKETS - 1][None, :]) * LOG2E
    by_dist = rb[_t5_bucket(jnp.arange(2 * BLOCK, dtype=jnp.int32))].T
    r = jnp.arange(BLOCK, dtype=jnp.int32)[:, None]
    c = jnp.arange(BLOCK, dtype=jnp.int32)[None, :]
    d0 = jnp.where((r >= c)[None], by_dist[:, jnp.maximum(r - c, 0)], NEG_INF)
    d1 = by_dist[:, BLOCK + r - c]
    return d0, d1


def _tile_bias(d0, d1, nb, diagonal):
    zero = jnp.zeros((BLOCK, BLOCK), F32)
    neg = jnp.full((BLOCK, BLOCK), NEG_INF, F32)
    rows = []
    for a in range(nb):
        if diagonal:
            blocks = [d0 if c == a else d1 if c == a - 1 else neg if c > a else zero for c in range(nb)]
        else:
            blocks = [d1 if (a == 0 and c == nb - 1) else zero for c in range(nb)]
        rows.append(jnp.concatenate(blocks, axis=1))
    return jnp.concatenate(rows, axis=0)


def _attn_kernel(qi_tab, ki_tab, lam_ref, q_ref, k_ref, v_ref, d0_ref, d1_ref, sg_ref, o_ref,
                 m_sc, l_sc, acc_sc, *, bq, bk, hpb):
    p = pl.program_id(2)
    qi = qi_tab[p]
    ki = ki_tab[p]
    nlb = bk // LANES

    @pl.when(ki == 0)
    def _():
        m_sc[...] = jnp.full(m_sc.shape, NEG_INF, BF16).astype(F32)
        l_sc[...] = jnp.zeros(l_sc.shape, F32)
        acc_sc[...] = jnp.zeros(acc_sc.shape, F32)

    def scores(hh, m):
        sl = slice(hh * ATT_VD + m * ATT_HD, hh * ATT_VD + (m + 1) * ATT_HD)
        return lax.dot_general(q_ref[0, :, sl], k_ref[0, :, sl], (((1,), (1,)), ((), ())),
                               preferred_element_type=F32)

    def update(hh, m, s):
        c_idx = 2 * hh + m
        sb = s.astype(BF16)
        lane_blocks = [sb[:, c * LANES:(c + 1) * LANES] for c in range(nlb)]
        m_old = m_sc[c_idx]
        blk_max = functools.reduce(jnp.maximum, lane_blocks).astype(F32)
        m_new = jnp.maximum(m_old, jnp.max(blk_max, axis=-1, keepdims=True))
        alpha = jnp.exp2(m_old - m_new)
        m_b = m_new.astype(BF16)
        pbs = [jnp.exp2(lb - m_b) for lb in lane_blocks]
        l_sc[c_idx] = alpha * l_sc[c_idx] + functools.reduce(jnp.add, pbs).astype(F32)
        acc_sc[c_idx] = jnp.concatenate([alpha] * (ATT_VD // LANES), axis=1) * acc_sc[c_idx] + jnp.dot(
            jnp.concatenate(pbs, axis=1), v_ref[0, :, hh * ATT_VD:(hh + 1) * ATT_VD],
            preferred_element_type=F32)
        m_sc[c_idx] = m_new

    def key_valid():
        col = ki * bk + lax.broadcasted_iota(jnp.int32, (1, bk), 1)
        return col >= PAD

    def sweep(bias_of, masked):
        ok = key_valid() if masked else None
        for hh in range(hpb):
            bias = bias_of(hh)
            for m in range(2):
                s = scores(hh, m)
                if bias is not None:
                    s = s + bias
                if masked:
                    s = jnp.where(ok, s, NEG_INF)
                update(hh, m, s)

    @pl.when(ki == qi)
    def _():
        sweep(lambda hh: _tile_bias(d0_ref[hh], d1_ref[hh], bk // BLOCK, True), True)
        lam = lam_ref[0, 0]
        for hh in range(hpb):
            l0 = jnp.sum(l_sc[2 * hh], axis=-1, keepdims=True)
            l1 = jnp.sum(l_sc[2 * hh + 1], axis=-1, keepdims=True)
            o = acc_sc[2 * hh] / l0 - lam * (acc_sc[2 * hh + 1] / l1)
            ms = jnp.mean(o * o, axis=-1, keepdims=True)
            o_ref[:, hh * ATT_VD:(hh + 1) * ATT_VD] = (o * lax.rsqrt(ms + RMS_EPS) * sg_ref[...]).astype(BF16)

    @pl.when(ki == qi - 1)
    def _():
        sweep(lambda hh: _tile_bias(d0_ref[hh], d1_ref[hh], bk // BLOCK, False), True)

    @pl.when((ki < qi - 1) & (ki == 0))
    def _():
        sweep(lambda hh: None, True)

    @pl.when((ki < qi - 1) & (ki > 0))
    def _():
        sweep(lambda hh: None, False)


def _diff_attention(main, lam, pat_d0, pat_d1, subln, batch, lp, blk):
    _, rp, d = main.shape
    nb = lp // blk
    hpb = 4
    wb = hpb * ATT_VD
    pairs =[(qi, ki) for qi in range(nb) for ki in range(qi + 1)]
    qi_tab = jnp.asarray([a for a, _ in pairs], jnp.int32)
    ki_tab = jnp.asarray([b for _, b in pairs], jnp.int32)
    kern = functools.partial(_attn_kernel, bq=blk, bk=blk, hpb=hpb)
    grid_spec = pltpu.PrefetchScalarGridSpec(
        num_scalar_prefetch=2,
        grid=(batch, ATT_HEADS // hpb, len(pairs)),
        in_specs=[
            pl.BlockSpec(memory_space=pltpu.SMEM),
            pl.BlockSpec((1, blk, wb), lambda b, h, p, qt, kt: (0, b * nb + qt[p], h)),
            pl.BlockSpec((1, blk, wb), lambda b, h, p, qt, kt: (1, b * nb + kt[p], h)),
            pl.BlockSpec((1, blk, wb), lambda b, h, p, qt, kt: (2, b * nb + kt[p], h)),
            pl.BlockSpec((hpb, BLOCK, BLOCK), lambda b, h, p, qt, kt: (h, 0, 0)),
            pl.BlockSpec((hpb, BLOCK, BLOCK), lambda b, h, p, qt, kt: (h, 0, 0)),
            pl.BlockSpec((1, ATT_VD), lambda b, h, p, qt, kt: (0, 0)),
        ],
        out_specs=pl.BlockSpec((blk, wb), lambda b, h, p, qt, kt: (b * nb + qt[p], h)),
        scratch_shapes=[
            pltpu.VMEM((2 * hpb, blk, LANES), F32),
            pltpu.VMEM((2 * hpb, blk, LANES), F32),
            pltpu.VMEM((2 * hpb, blk, ATT_VD), F32),
        ],
    )
    return pl.pallas_call(
        kern,
        grid_spec=grid_spec,
        out_shape=jax.ShapeDtypeStruct((rp, d), BF16),
        compiler_params=pltpu.CompilerParams(
            dimension_semantics=("arbitrary", "arbitrary", "arbitrary"), vmem_limit_bytes=VMEM_LIMIT),
        name="diff_attn",
    )(qi_tab, ki_tab, lam, main, main, main, pat_d0, pat_d1, subln)


def _hgrn_kernel(q_ref, k_ref, v_ref, g_ref, gn_ref, o_ref, st_sc, b_sc, kf_sc, vf_sc, *, tc, hb):
    c_len = HG_CHUNK
    n_sub = HG_CHUNK // HG_SUB
    half = SUBLANES

    @pl.when(pl.program_id(2) == 0)
    def _():
        st_sc[...] = jnp.zeros(st_sc.shape, F32)

    ri = lax.broadcasted_iota(jnp.int32, (c_len, c_len), 0)
    ci = lax.broadcasted_iota(jnp.int32, (c_len, c_len), 1)
    tri = (ri >= ci).astype(F32)
    sub_r = ri // HG_SUB
    sub_c = ci // HG_SUB
    r2 = lax.broadcasted_iota(jnp.int32, (2 * LANES, 2 * LANES), 0) // LANES
    c2 = lax.broadcasted_iota(jnp.int32, (2 * LANES, 2 * LANES), 1) // LANES
    ones_bd = (r2 == c2).astype(BF16)
    row8 = lax.broadcasted_iota(jnp.int32, (half, LANES), 0)
    nt = (((1,), (1,)), ((), ()))

    def head(hh, r0, b_all):
        hs = slice(hh * HG_DK, (hh + 1) * HG_DK)
        q = q_ref[0, pl.ds(r0, c_len), hs].astype(F32)
        k = kf_sc[:, hs]
        v_bf = v_ref[0, pl.ds(r0, c_len), hs]
        b = b_all[:, hs]
        b_end = b[c_len - 1:c_len, :]
        st = st_sc[hh]

        o = lax.dot_general((q * jnp.exp2(b)).astype(BF16), st.astype(BF16), nt, preferred_element_type=F32)

        e_rows = [b[(j + 1) * HG_SUB - 1:(j + 1) * HG_SUB, :] for j in range(n_sub)]
        e_full = jnp.concatenate([jnp.broadcast_to(e, (HG_SUB, LANES)) for e in e_rows], axis=0)
        k_rel = (k * jnp.exp2(e_full - b)).astype(BF16)
        q_rel = jnp.concatenate(
            [(q * jnp.exp2(jnp.minimum(b - e_rows[j], 0.0))).astype(BF16) for j in range(n_sub - 1)], axis=0)
        a_all = lax.dot_general(q_rel, k_rel, nt, preferred_element_type=F32)
        a_off = jnp.zeros((c_len, c_len), F32)
        for j in range(n_sub - 1):
            a_off = jnp.where(sub_c == j, a_all[j * c_len:(j + 1) * c_len], a_off)
        a_off = jnp.where(sub_r > sub_c, a_off, 0.0)
        o = o + jnp.dot(a_off.astype(BF16), v_bf, preferred_element_type=F32)

        pieces = []
        meta = []
        for i in range(n_sub):
            for s in range(HG_SUB):
                row = i * HG_SUB + s
                ks = kf_sc[pl.ds(row, 1), hs]
                bs = b_sc[pl.ds(row, 1), hs]
                for hf in range(HG_SUB // half):
                    t0 = hf * half
                    if t0 + half - 1 < s:
                        continue
                    rows = slice(i * HG_SUB + t0, i * HG_SUB + t0 + half)
                    if s > t0:
                        w = q[rows] * ks * jnp.exp2(jnp.minimum(b[rows] - bs, 0.0))
                        w = jnp.where(row8 + t0 >= s, w, 0.0)
                    else:
                        w = q[rows] * ks * jnp.exp2(b[rows] - bs)
                    pieces.append(w.astype(BF16))
                    meta.append((i, s, hf))
        n_pairs = len(pieces) // 2
        lhs = jnp.concatenate(
            [jnp.concatenate([pieces[2 * n], pieces[2 * n + 1]], axis=1) for n in range(n_pairs)], axis=0)
        sums = jnp.dot(lhs, ones_bd, preferred_element_type=F32)
        diag = [[jnp.zeros((half, LANES), F32) for _ in range(HG_SUB // half)] for _ in range(n_sub)]
        for n, (i, s, hf) in enumerate(meta):
            blk = sums[(n // 2) * half:(n // 2 + 1) * half, (n % 2) * LANES:(n % 2 + 1) * LANES]
            vs = vf_sc[pl.ds(i * HG_SUB + s, 1), hs]
            diag[i][hf] = diag[i][hf] + blk * vs
        o = o + jnp.concatenate([d for row in diag for d in row], axis=0)

        k_out = (k * jnp.exp2(b_end - b)).astype(BF16)
        st_sc[hh] = st * jnp.exp2(b_end) + lax.dot_general(
            v_bf, k_out, (((0,), (0,)), ((), ())), preferred_element_type=F32)

        ms = jnp.mean(o * o, axis=-1, keepdims=True)
        o_ref[pl.ds(r0, c_len), hs] = (o * lax.rsqrt(ms + RMS_EPS) * gn_ref[...]).astype(BF16)

    def chunk(c, carry):
        r0 = pl.multiple_of(c * c_len, c_len)
        b_all = jnp.dot(tri, g_ref[pl.ds(r0, c_len), :], precision=lax.Precision.HIGHEST,
                        preferred_element_type=F32)
        b_sc[...] = b_all
        kf_sc[...] = k_ref[0, pl.ds(r0, c_len), :].astype(F32)
        vf_sc[...] = v_ref[0, pl.ds(r0, c_len), :].astype(F32)
        for hh in range(hb):
            head(hh, r0, b_all)
        return carry

    lax.fori_loop(0, tc // c_len, chunk, 0)


def _hgrn2(main, g, gn, batch, lp):
    _, rp, d = main.shape
    tc = _largest_divisor(lp, (640, 512, 256, 128, 64))
    nt_ = lp // tc
    hb = 16
    wb = hb * HG_DK
    kern = functools.partial(_hgrn_kernel, tc=tc, hb=hb)

    def seg_spec(seg):
        return pl.BlockSpec((1, tc, wb), lambda b, h, t: (seg, b * nt_ + t, h))

    return pl.pallas_call(
        kern,
        grid=(batch, HG_HEADS // hb, nt_),
        in_specs=[
            seg_spec(3), seg_spec(4), seg_spec(5),
            pl.BlockSpec((tc, wb), lambda b, h, t: (b * nt_ + t, h)),
            pl.BlockSpec((1, HG_DK), lambda b, h, t: (0, 0)),
        ],
        out_specs=pl.BlockSpec((tc, wb), lambda b, h, t: (b * nt_ + t, h)),
        out_shape=jax.ShapeDtypeStruct((rp, d), BF16),
        scratch_shapes=[
            pltpu.VMEM((hb, HG_DK, HG_DK), F32),
            pltpu.VMEM((HG_CHUNK, wb), F32),
            pltpu.VMEM((HG_CHUNK, wb), F32),
            pltpu.VMEM((HG_CHUNK, wb), F32),
        ],
        compiler_params=pltpu.CompilerParams(
            dimension_semantics=("arbitrary", "arbitrary", "arbitrary"), vmem_limit_bytes=VMEM_LIMIT),
        name="hgrn2",
    )(main, main, main, g, gn)


def _outproj_kernel(ga_ref, gh_ref, og_ref, oa_ref, oh_ref, h_ref, w_ref, n2_ref, rw_ref, rb_ref,
                    h1_ref, u2_ref, ti_ref, gt_ref):
    tm = h_ref.shape[0]
    y = (ga_ref[0].astype(F32) * oa_ref[...].astype(F32)
         + gh_ref[0].astype(F32) * (oh_ref[...].astype(F32) * og_ref[0].astype(F32)))
    h1 = h_ref[...] + jnp.dot(y.astype(BF16), w_ref[...], preferred_element_type=F32)
    h1_ref[...] = h1
    ms = jnp.mean(h1 * h1, axis=-1, keepdims=True)
    u2 = h1 * lax.rsqrt(ms + RMS_EPS) * n2_ref[...]
    for s in range(ROW_TILE):
        u2_ref[pl.ds(s, tm, stride=ROW_TILE), :] = u2[:, s * LANES:(s + 1) * LANES]
    logits = jnp.dot(u2, rw_ref[...], precision=lax.Precision.HIGHEST,
                     preferred_element_type=F32) + rb_ref[...]
    lane = lax.broadcasted_iota(jnp.int32, logits.shape, 1)
    cur = logits
    vals, idxs = [], []
    for _ in range(TOP_K):
        mx = jnp.max(cur, axis=-1, keepdims=True)
        ix = jnp.min(jnp.where(cur == mx, lane, LANES), axis=-1, keepdims=True)
        vals.append(mx)
        idxs.append(ix)
        cur = jnp.where(lane == ix, -jnp.inf, cur)
    es = [jnp.exp(v - vals[0]) for v in vals]
    inv = 1.0 / (es[0] + es[1] + es[2] + es[3])
    ti = jnp.zeros(logits.shape, jnp.int32)
    gt = jnp.zeros(logits.shape, F32)
    for kk in range(TOP_K):
        ti = jnp.where(lane == kk, idxs[kk], ti)
        gt = jnp.where(lane == kk, es[kk] * inv, gt)
    ti_ref[...] = ti
    gt_ref[...] = gt


def _out_proj(main, o_att, o_hg, h_pad, w_out_bf, n2, rw_pad, rb_pad):
    rp, d = h_pad.shape
    tm = _largest_divisor(rp, (256, 128))

    def seg_spec(seg):
        return pl.BlockSpec((1, tm, d), lambda i: (seg, i, 0))

    row = pl.BlockSpec((tm, d), lambda i: (i, 0))
    small = pl.BlockSpec((tm, LANES), lambda i: (i, 0))
    return pl.pallas_call(
        _outproj_kernel,
        grid=(rp // tm,),
        in_specs=[
            seg_spec(7), seg_spec(8), seg_spec(6), row, row, row,
            pl.BlockSpec((d, d), lambda i: (0, 0)),
            pl.BlockSpec((1, d), lambda i: (0, 0)),
            pl.BlockSpec((d, LANES), lambda i: (0, 0)),
            pl.BlockSpec((1, LANES), lambda i: (0, 0)),
        ],
        out_specs=[row, pl.BlockSpec((tm * ROW_TILE, LANES), lambda i: (i, 0)), small, small],
        out_shape=[
            jax.ShapeDtypeStruct((rp, d), F32),
            jax.ShapeDtypeStruct((rp * ROW_TILE, LANES), F32),
            jax.ShapeDtypeStruct((rp, LANES), jnp.int32),
            jax.ShapeDtypeStruct((rp, LANES), F32),
        ],
        compiler_params=pltpu.CompilerParams(
            dimension_semantics=("arbitrary",), vmem_limit_bytes=VMEM_LIMIT),
        name="out_proj",
    )(main, main, main, o_att, o_hg, h_pad, w_out_bf, n2, rw_pad, rb_pad)


def _rank_kernel(ti_ref, rank_ref, cnt_ref, carry_sc, *, tm, lp):
    i = pl.program_id(0)

    @pl.when(i == 0)
    def _():
        carry_sc[...] = jnp.zeros(carry_sc.shape, F32)

    ti = ti_ref[...]
    lane = lax.broadcasted_iota(jnp.int32, (tm, LANES), 1)
    row = i * tm + lax.broadcasted_iota(jnp.int32, (tm, 1), 0)
    valid = lax.rem(row, lp) >= PAD
    hot = [(lane == ti[:, kk:kk + 1]) & valid for kk in range(TOP_K)]
    any_hot = hot[0] | hot[1] | hot[2] | hot[3]
    any_f = jnp.where(any_hot, 1.0, 0.0)
    rr = lax.broadcasted_iota(jnp.int32, (tm, tm), 0)
    cc = lax.broadcasted_iota(jnp.int32, (tm, tm), 1)
    strict = jnp.where(rr > cc, 1.0, 0.0).astype(BF16)
    base = carry_sc[...] + jnp.dot(strict, any_f.astype(BF16), preferred_element_type=F32)
    rank = jnp.zeros((tm, LANES), F32)
    for kk in range(TOP_K):
        rk = jnp.sum(jnp.where(hot[kk], base, 0.0), axis=-1, keepdims=True)
        rank = jnp.where(lane == kk, rk, rank)
    rank_ref[...] = rank.astype(jnp.int32)
    carry_sc[...] = carry_sc[...] + jnp.sum(any_f, axis=0, keepdims=True)
    cnt_ref[...] = carry_sc[...]


def _moe_rank(topi, lp):
    rp = topi.shape[0]
    tm = _largest_divisor(rp, (256, 128))
    kern = functools.partial(_rank_kernel, tm=tm, lp=lp)
    return pl.pallas_call(
        kern,
        grid=(rp // tm,),
        in_specs=[pl.BlockSpec((tm, LANES), lambda i: (i, 0))],
        out_specs=[pl.BlockSpec((tm, LANES), lambda i: (i, 0)), pl.BlockSpec((1, LANES), lambda i: (0, 0))],
        out_shape=[jax.ShapeDtypeStruct((rp, LANES), jnp.int32), jax.ShapeDtypeStruct((1, LANES), F32)],
        scratch_shapes=[pltpu.VMEM((1, LANES), F32)],
        compiler_params=pltpu.CompilerParams(dimension_semantics=("arbitrary",)),
        name="moe_rank",
    )(topi)


def _row_gather_start(src_hbm, dst, sem, tok, r):
    pltpu.make_async_copy(
        src_hbm.at[pl.ds(pl.multiple_of(tok * ROW_TILE, ROW_TILE), ROW_TILE), :],
        dst.at[pl.ds(pl.multiple_of(r * ROW_TILE, ROW_TILE), ROW_TILE), :],
        sem).start()


def _ffn_kernel(bexp_ref, nact_ref, tokc_ref, tokn_ref, u2_hbm, wgu_hbm, wdn_hbm, bgu_ref, bdn_ref, ys_ref,
                gbuf, ring, gu_sc, acc_sc, gsem, wsem, *, rb, d, dff, tk):
    i = pl.program_id(0)
    nact = nact_ref[0]
    slot = lax.rem(i, 2)
    nka = d // tk
    nkb = dff // tk
    n_chunks = 2 * nka + nkb
    n_ring = ring.shape[0]
    assert n_chunks % n_ring == 0 and dff == d

    def chunk_copy(blk, c, ring_slot):
        e = bexp_ref[blk]
        if c < 2 * nka:
            kt, half = divmod(c, 2)
            src = wgu_hbm.at[e, pl.ds(kt * tk, tk), pl.ds(half * dff, dff)]
        else:
            src = wdn_hbm.at[e, pl.ds((c - 2 * nka) * tk, tk), :]
        return pltpu.make_async_copy(src, ring.at[ring_slot], wsem.at[ring_slot])

    def issue_rows(tok_ref, sl):
        def body(r, carry):
            _row_gather_start(u2_hbm, gbuf.at[sl], gsem.at[sl], tok_ref[0, 0, r], r)
            return carry
        lax.fori_loop(0, rb, body, 0, unroll=8)

    @pl.when(i == 0)
    def _():
        issue_rows(tokc_ref, 0)
        for c in range(n_ring - 1):
            chunk_copy(0, c, c).start()

    @pl.when(i + 1 < nact)
    def _():
        issue_rows(tokn_ref, 1 - slot)

    @pl.when(i < nact)
    def _():
        pltpu.make_async_copy(gbuf.at[slot], gbuf.at[slot], gsem.at[slot]).wait()
        gu_sc[...] = jnp.broadcast_to(bgu_ref[0], gu_sc.shape)
        acc_sc[...] = jnp.broadcast_to(bdn_ref[0], acc_sc.shape)
        xk = None
        for c in range(n_chunks):
            rs = c % n_ring
            chunk_copy(i, c, rs).wait()
            nxt = c + n_ring - 1
            if nxt < n_chunks:
                chunk_copy(i, nxt, nxt % n_ring).start()
            else:
                chunk_copy(i + 1, nxt - n_chunks, nxt % n_ring).start()
            w = ring[rs].astype(BF16)
            if c < 2 * nka:
                kt, half = divmod(c, 2)
                if half == 0:
                    per = tk // LANES
                    xk = jnp.concatenate(
                        [gbuf[slot, pl.ds(kt * per + s, rb, stride=ROW_TILE), :].astype(BF16)
                         for s in range(per)], axis=1)
                cols = slice(half * dff, (half + 1) * dff)
                gu_sc[:, cols] += jnp.dot(xk, w, preferred_element_type=F32)
            else:
                kt = c - 2 * nka
                xg = jnp.minimum(gu_sc[:, kt * tk:(kt + 1) * tk], SWIGLU_LIMIT)
                xl = jnp.clip(gu_sc[:, dff + kt * tk:dff + (kt + 1) * tk], -SWIGLU_LIMIT, SWIGLU_LIMIT)
                act = (xg * _sigmoid(SWIGLU_ALPHA * xg) * (xl + 1.0)).astype(BF16)
                acc_sc[...] += jnp.dot(act, w, preferred_element_type=F32)
        for s in range(ROW_TILE):
            ys_ref[pl.ds(s, rb, stride=ROW_TILE), :] = acc_sc[:, s * LANES:(s + 1) * LANES]

    @pl.when(i == nact)
    def _():
        for c in range(n_ring - 1):
            chunk_copy(i, c, c).wait()

    @pl.when(i >= nact)
    def _():
        ys_ref[...] = jnp.zeros(ys_ref.shape, F32)


def _moe_ffn(block_exp, nact, slot_tok, u2_rows, w_gu, b_gu, w_dn, b_dn, rb):
    n_e, d, f2 = w_gu.shape
    dff = f2 // 2
    nblk = slot_tok.shape[0]
    tk = 512
    n_ring = 4
    kern = functools.partial(_ffn_kernel, rb=rb, d=d, dff=dff, tk=tk)
    grid_spec = pltpu.PrefetchScalarGridSpec(
        num_scalar_prefetch=2,
        grid=(nblk,),
        in_specs=[
            pl.BlockSpec((1, 1, rb), lambda i, be, na: (i, 0, 0), memory_space=pltpu.SMEM),
            pl.BlockSpec((1, 1, rb), lambda i, be, na: (jnp.minimum(i + 1, nblk - 1), 0, 0),
                         memory_space=pltpu.SMEM),
            pl.BlockSpec(memory_space=pl.ANY),
            pl.BlockSpec(memory_space=pl.ANY),
            pl.BlockSpec(memory_space=pl.ANY),
            pl.BlockSpec((1, 1, f2), lambda i, be, na: (be[i], 0, 0)),
            pl.BlockSpec((1, 1, d), lambda i, be, na: (be[i], 0, 0)),
        ],
        out_specs=pl.BlockSpec((rb * ROW_TILE, LANES), lambda i, be, na: (i, 0)),
        scratch_shapes=[
            pltpu.VMEM((2, rb * ROW_TILE, LANES), F32),
            pltpu.VMEM((n_ring, tk, d), F32),
            pltpu.VMEM((rb, f2), F32),
            pltpu.VMEM((rb, d), F32),
            pltpu.SemaphoreType.DMA((2,)),
            pltpu.SemaphoreType.DMA((n_ring,)),
        ],
    )
    return pl.pallas_call(
        kern,
        grid_spec=grid_spec,
        out_shape=jax.ShapeDtypeStruct((nblk * rb * ROW_TILE, LANES), F32),
        compiler_params=pltpu.CompilerParams(
            dimension_semantics=("arbitrary",), vmem_limit_bytes=VMEM_LIMIT),
        name="moe_ffn",
    )(block_exp, nact, slot_tok, slot_tok, u2_rows, w_gu, w_dn, b_gu.reshape(n_e, 1, f2),
      b_dn.reshape(n_e, 1, d))


def _combine_kernel(dc_ref, dn_ref, ys_hbm, gt_ref, h1_ref, o_ref, cbuf, sem, *, tm, n_steps, n_i):
    n = pl.program_id(0) * n_i + pl.program_id(1)
    slot = lax.rem(n, 2)

    def issue(dref, sl):
        def body(r, carry):
            for kk in range(TOP_K):
                _row_gather_start(ys_hbm, cbuf.at[sl, kk], sem.at[sl], dref[0, 0, r * TOP_K + kk], r)
            return carry
        lax.fori_loop(0, tm, body, 0, unroll=4)

    @pl.when(n == 0)
    def _():
        issue(dc_ref, 0)

    @pl.when(n + 1 < n_steps)
    def _():
        issue(dn_ref, 1 - slot)

    pltpu.make_async_copy(cbuf.at[slot], cbuf.at[slot], sem.at[slot]).wait()
    gates = [jnp.broadcast_to(gt_ref[:, kk:kk + 1], (tm, LANES)) for kk in range(TOP_K)]
    for s in range(ROW_TILE):
        sl = slice(s * LANES, (s + 1) * LANES)
        y = h1_ref[:, sl]
        for kk in range(TOP_K):
            y = y + gates[kk] * cbuf[slot, kk, pl.ds(s, tm, stride=ROW_TILE), :]
        o_ref[0, :, sl] = y


def _combine(dest_x, ys_rows, gate, h1, batch, seq, lp):
    rp, d = h1.shape
    tm = BLOCK
    n_i = seq // tm
    n_steps = batch * n_i
    nlp = lp // tm
    kern = functools.partial(_combine_kernel, tm=tm, n_steps=n_steps, n_i=n_i)
    return pl.pallas_call(
        kern,
        grid=(batch, n_i),
        in_specs=[
            pl.BlockSpec((1, 1, tm * TOP_K), lambda b, i: (b * n_i + i, 0, 0), memory_space=pltpu.SMEM),
            pl.BlockSpec((1, 1, tm * TOP_K), lambda b, i: (jnp.minimum(b * n_i + i + 1, n_steps - 1), 0, 0),
                         memory_space=pltpu.SMEM),
            pl.BlockSpec(memory_space=pl.ANY),
            pl.BlockSpec((tm, LANES), lambda b, i: (b * nlp + 1 + i, 0)),
            pl.BlockSpec((tm, d), lambda b, i: (b * nlp + 1 + i, 0)),
        ],
        out_specs=pl.BlockSpec((1, tm, d), lambda b, i: (b, i, 0)),
        out_shape=jax.ShapeDtypeStruct((batch, seq, d), F32),
        scratch_shapes=[
            pltpu.VMEM((2, TOP_K, tm * ROW_TILE, LANES), F32),
            pltpu.SemaphoreType.DMA((2,)),
        ],
        compiler_params=pltpu.CompilerParams(
            dimension_semantics=("arbitrary", "arbitrary"), vmem_limit_bytes=VMEM_LIMIT),
        name="moe_combine",
    )(dest_x, dest_x, ys_rows, gate, h1)


def kernel(x, meta_tokens, rel_bias, lb_logits, norm1, w_in, q_norm, k_norm, diff_lambda, diff_subln,
           hgrn_norm, w_out, norm2, router_w, router_b, w_gate_up, b_gate_up, w_down, b_down):
    batch, seq, d = x.shape
    assert norm1.shape[0] == 1, "single-layer block"
    assert d == ATT_HEADS * ATT_VD == HG_HEADS * HG_DK == ROW_TILE * LANES and seq % BLOCK == 0
    lp = seq + BLOCK
    rp = batch * lp

    first = jnp.concatenate([jnp.zeros((PAD, d), x.dtype), meta_tokens.astype(x.dtype)], axis=0)
    h_pad = jnp.concatenate([jnp.broadcast_to(first[None], (batch, BLOCK, d)), x], axis=1).reshape(rp, d)

    lb = jax.nn.softmax(lb_logits.astype(F32), axis=0)[0]
    lv = diff_lambda[0].astype(F32)
    lam = (jnp.exp(jnp.sum(lv[0] * lv[1])) - jnp.exp(jnp.sum(lv[2] * lv[3])) + LAMBDA_INIT).reshape(1, 1)
    n_grp = d // ATT_HD
    gains = jnp.concatenate([jnp.tile(q_norm[0].astype(F32) * (ATT_HD ** -0.5 * LOG2E), n_grp),
                             jnp.tile(k_norm[0].astype(F32), n_grp),
                             jnp.ones(((N_SEG - 2) * d,), F32)])
    lbs = jnp.concatenate([jnp.zeros((4 * d,), F32), lb, jnp.zeros((4 * d,), F32)])
    colp = jnp.stack([gains, lbs])

    main, g = _in_proj(h_pad, norm1.astype(F32), w_in[0].astype(BF16), colp)

    blk = _largest_divisor(lp, (640, 512, 384, 256, 128))
    pat_d0, pat_d1 = _bias_patterns(rel_bias)
    subln = (diff_subln[0].astype(F32) * (1.0 - LAMBDA_INIT)).reshape(1, ATT_VD)
    o_att = _diff_attention(main, lam, pat_d0, pat_d1, subln, batch, lp, blk)

    o_hg = _hgrn2(main, g, hgrn_norm.astype(F32).reshape(1, HG_DK), batch, lp)

    rw_pad = jnp.pad(router_w[0].astype(F32), ((0, 0), (0, LANES - N_EXPERTS)))
    rb_pad = jnp.pad(router_b[0].astype(F32), (0, LANES - N_EXPERTS), constant_values=NEG_INF).reshape(1, LANES)
    h1, u2_rows, topi, gate = _out_proj(main, o_att, o_hg, h_pad, w_out[0].astype(BF16), norm2.astype(F32),
                                        rw_pad, rb_pad)

    rank, counts = _moe_rank(topi, lp)

    rb = 512
    n_assign = batch * (seq + N_META) * TOP_K
    nblk = n_assign // rb + N_EXPERTS + 1
    cnt = counts[0, :N_EXPERTS].astype(jnp.int32)
    padded = (cnt + rb - 1) // rb * rb
    pend = jnp.cumsum(padded)
    pstart = pend - padded
    rows = jnp.arange(rp, dtype=jnp.int32)
    valid = (rows % lp) >= PAD
    ti4 = topi[:, :TOP_K]
    dest = pstart[ti4] + rank[:, :TOP_K]
    oob = nblk * rb + rows[:, None] * TOP_K + jnp.arange(TOP_K, dtype=jnp.int32)[None, :]
    dest_s = jnp.where(valid[:, None], dest, oob)
    slot_tok = jnp.zeros((nblk * rb,), jnp.int32).at[dest_s.reshape(-1)].set(
        jnp.repeat(rows, TOP_K), mode="drop", unique_indices=True)
    nact = (pend[-1] // rb).astype(jnp.int32)
    blk_ids = jnp.minimum(jnp.arange(nblk, dtype=jnp.int32), nact - 1)
    block_exp = jnp.minimum(jnp.sum((pend[None, :] <= (blk_ids * rb)[:, None]).astype(jnp.int32), axis=1),
                            N_EXPERTS - 1)

    ys_rows = _moe_ffn(block_exp, nact.reshape(1), slot_tok.reshape(nblk, 1, rb), u2_rows,
                       w_gate_up[0], b_gate_up[0].astype(F32), w_down[0], b_down[0].astype(F32), rb)

    dest_x = dest.reshape(batch, lp, TOP_K)[:, BLOCK:, :].reshape(batch * seq // BLOCK, 1, BLOCK * TOP_K)
    return _combine(dest_x, ys_rows, gate, h1, batch, seq, lp)
```

```python
import functools
import math

import jax
import jax.numpy as jnp
from jax import lax
from jax.experimental import pallas as pl
from jax.experimental.pallas import tpu as pltpu

N_META = 16
BLOCK = 128
PAD = BLOCK - N_META

ATT_HEADS = 8
ATT_HD = 128
ATT_VD = 2 * ATT_HD

HG_HEADS = 16
HG_DK = 128
HG_CHUNK = 64
HG_SUB = 16

REL_BUCKETS = 32
REL_MAX_DIST = 128

N_EXPERTS = 32
TOP_K = 4
SWIGLU_LIMIT = 7.0
SWIGLU_ALPHA = 1.702

RMS_EPS = 1e-6
NEG_INF = -1e30
LAMBDA_INIT = 0.8 - 0.6 * math.exp(-0.3 * 0)
LOG2E = 1.0 / math.log(2.0)

N_SEG = 9
LANES = 128
SUBLANES = 8
ROW_TILE = 16
VMEM_LIMIT = 56 * 1024 * 1024

F32 = jnp.float32
BF16 = jnp.bfloat16


def _largest_divisor(n, candidates):
    for c in candidates:
        if n % c == 0:
            return c
    raise ValueError(f"no tile in {candidates} divides {n}")


def _sigmoid(x):
    return 1.0 / (1.0 + jnp.exp(-x))


def _inproj_kernel(x_ref, n1_ref, w_ref, cp_ref, main_ref, g_ref, u_ref, *, tq):
    seg = pl.program_id(1)

    @pl.when(seg == 0)
    def _():
        xf = x_ref[...]
        ms = jnp.mean(xf * xf, axis=-1, keepdims=True)
        u_ref[...] = (xf * lax.rsqrt(ms + RMS_EPS) * n1_ref[...]).astype(BF16)

    def slabs(epilogue):
        for q in range(w_ref.shape[1] // tq):
            cs = slice(q * tq, (q + 1) * tq)
            epilogue(jnp.dot(u_ref[...], w_ref[:, cs], preferred_element_type=F32), cs)

    @pl.when(seg <= 1)
    def _():
        def epilogue(acc, cs):
            for gi in range(tq // LANES):
                sl = slice(cs.start + gi * LANES, cs.start + (gi + 1) * LANES)
                y = acc[:, gi * LANES:(gi + 1) * LANES]
                ms = jnp.mean(y * y, axis=-1, keepdims=True)
                main_ref[0, :, sl] = (y * lax.rsqrt(ms + RMS_EPS) * cp_ref[0:1, sl]).astype(BF16)
        slabs(epilogue)

    @pl.when((seg == 2) | (seg == 5))
    def _():
        def epilogue(acc, cs):
            main_ref[0, :, cs] = acc.astype(BF16)
        slabs(epilogue)

    @pl.when((seg == 3) | (seg == 6))
    def _():
        def epilogue(acc, cs):
            main_ref[0, :, cs] = (acc * _sigmoid(acc)).astype(BF16)
        slabs(epilogue)

    @pl.when(seg == 4)
    def _():
        def epilogue(acc, cs):
            lb = cp_ref[1:2, cs]
            sg = _sigmoid(acc)
            g_ref[:, cs] = jnp.log(lb + (1.0 - lb) * sg) * LOG2E
            main_ref[0, :, cs] = ((1.0 - lb) * (1.0 - sg)).astype(BF16)
        slabs(epilogue)

    @pl.when(seg >= 7)
    def _():
        def epilogue(acc, cs):
            main_ref[0, :, cs] = _sigmoid(acc).astype(BF16)
        slabs(epilogue)


def _in_proj(h_pad, n1, w_in_bf, colp):
    rp, d = h_pad.shape
    tm = _largest_divisor(rp, (512, 256, 128))
    kern = functools.partial(_inproj_kernel, tq=512)
    return pl.pallas_call(
        kern,
        grid=(rp // tm, N_SEG),
        in_specs=[
            pl.BlockSpec((tm, d), lambda i, j: (i, 0)),
            pl.BlockSpec((1, d), lambda i, j: (0, 0)),
            pl.BlockSpec((d, d), lambda i, j: (0, j)),
            pl.BlockSpec((2, d), lambda i, j: (0, j)),
        ],
        out_specs=[
            pl.BlockSpec((1, tm, d), lambda i, j: (j, i, 0)),
            pl.BlockSpec((tm, d), lambda i, j: (i, 0)),
        ],
        out_shape=[
            jax.ShapeDtypeStruct((N_SEG, rp, d), BF16),
            jax.ShapeDtypeStruct((rp, d), F32),
        ],
        scratch_shapes=[pltpu.VMEM((tm, d), BF16)],
        compiler_params=pltpu.CompilerParams(
            dimension_semantics=("arbitrary", "arbitrary"), vmem_limit_bytes=VMEM_LIMIT),
        name="in_proj",
    )(h_pad, n1, w_in_bf, colp)


def _t5_bucket(rel):
    n = jnp.maximum(rel, 0)
    max_exact = REL_BUCKETS // 2
    nf = jnp.maximum(n, 1).astype(F32)
    large = max_exact + (jnp.log(nf / max_exact) / math.log(REL_MAX_DIST / max_exact)
                         * (REL_BUCKETS - max_exact)).astype(jnp.int32)
    large = jnp.minimum(large, REL_BUCKETS - 1)
    return jnp.where(n < max_exact, n, large)


def _bias_patterns(rel_bias):
    rb =(rel_bias.astype(F32) - rel_bias.astype(F32)[REL_BUCKETS - 1][None, :]) * LOG2E
    by_dist = rb[_t5_bucket(jnp.arange(2 * BLOCK, dtype=jnp.int32))].T
    r = jnp.arange(BLOCK, dtype=jnp.int32)[:, None]
    c = jnp.arange(BLOCK, dtype=jnp.int32)[None, :]
    d0 = jnp.where((r >= c)[None], by_dist[:, jnp.maximum(r - c, 0)], NEG_INF)
    d1 = by_dist[:, BLOCK + r - c]
    return d0, d1


def _tile_bias(d0, d1, nb, diagonal):
    zero = jnp.zeros((BLOCK, BLOCK), F32)
    neg = jnp.full((BLOCK, BLOCK), NEG_INF, F32)
    rows = []
    for a in range(nb):
        if diagonal:
            blocks = [d0 if c == a else d1 if c == a - 1 else neg if c > a else zero for c in range(nb)]
        else:
            blocks = [d1 if (a == 0 and c == nb - 1) else zero for c in range(nb)]
        rows.append(jnp.concatenate(blocks, axis=1))
    return jnp.concatenate(rows, axis=0)


def _attn_kernel(qi_tab, ki_tab, lam_ref, q_ref, k_ref, v_ref, d0_ref, d1_ref, sg_ref, o_ref,
                 m_sc, l_sc, acc_sc, *, bq, bk, hpb):
    p = pl.program_id(2)
    qi = qi_tab[p]
    ki = ki_tab[p]
    nlb = bk // LANES

    @pl.when(ki == 0)
    def _():
        m_sc[...] = jnp.full(m_sc.shape, NEG_INF, BF16).astype(F32)
        l_sc[...] = jnp.zeros(l_sc.shape, F32)
        acc_sc[...] = jnp.zeros(acc_sc.shape, F32)

    def scores(hh, m):
        sl = slice(hh * ATT_VD + m * ATT_HD, hh * ATT_VD + (m + 1) * ATT_HD)
        return lax.dot_general(q_ref[0, :, sl], k_ref[0, :, sl], (((1,), (1,)), ((), ())),
                               preferred_element_type=F32)

    def update(hh, m, s):
        c_idx = 2 * hh + m
        sb = s.astype(BF16)
        lane_blocks = [sb[:, c * LANES:(c + 1) * LANES] for c in range(nlb)]
        m_old = m_sc[c_idx]
        blk_max = functools.reduce(jnp.maximum, lane_blocks).astype(F32)
        m_new = jnp.maximum(m_old, jnp.max(blk_max, axis=-1, keepdims=True))
        alpha = jnp.exp2(m_old - m_new)
        m_b = m_new.astype(BF16)
        pbs = [jnp.exp2(lb - m_b) for lb in lane_blocks]
        l_sc[c_idx] = alpha * l_sc[c_idx] + functools.reduce(jnp.add, pbs).astype(F32)
        acc_sc[c_idx] = jnp.concatenate([alpha] * (ATT_VD // LANES), axis=1) * acc_sc[c_idx] + jnp.dot(
            jnp.concatenate(pbs, axis=1), v_ref[0, :, hh * ATT_VD:(hh + 1) * ATT_VD],
            preferred_element_type=F32)
        m_sc[c_idx] = m_new

    def key_valid():
        col = ki * bk + lax.broadcasted_iota(jnp.int32, (1, bk), 1)
        return col >= PAD

    def sweep(bias_of, masked):
        ok = key_valid() if masked else None
        for hh in range(hpb):
            bias = bias_of(hh)
            for m in range(2):
                s = scores(hh, m)
                if bias is not None:
                    s = s + bias
                if masked:
                    s = jnp.where(ok, s, NEG_INF)
                update(hh, m, s)

    @pl.when(ki == qi)
    def _():
        sweep(lambda hh: _tile_bias(d0_ref[hh], d1_ref[hh], bk // BLOCK, True), True)
        lam = lam_ref[0, 0]
        for hh in range(hpb):
            l0 = jnp.sum(l_sc[2 * hh], axis=-1, keepdims=True)
            l1 = jnp.sum(l_sc[2 * hh + 1], axis=-1, keepdims=True)
            o = acc_sc[2 * hh] / l0 - lam * (acc_sc[2 * hh + 1] / l1)
            ms = jnp.mean(o * o, axis=-1, keepdims=True)
            o_ref[:, hh * ATT_VD:(hh + 1) * ATT_VD] = (o * lax.rsqrt(ms + RMS_EPS) * sg_ref[...]).astype(BF16)

    @pl.when(ki == qi - 1)
    def _():
        sweep(lambda hh: _tile_bias(d0_ref[hh], d1_ref[hh], bk // BLOCK, False), True)

    @pl.when((ki < qi - 1) & (ki == 0))
    def _():
        sweep(lambda hh: None, True)

    @pl.when((ki < qi - 1) & (ki > 0))
    def _():
        sweep(lambda hh: None, False)


def _diff_attention(main, lam, pat_d0, pat_d1, subln, batch, lp, blk):
    _, rp, d = main.shape
    nb = lp // blk
    hpb = 8
    wb = hpb * ATT_VD
    pairs =[(qi, ki) for qi in range(nb) for ki in range(qi + 1)]
    qi_tab = jnp.asarray([a for a, _ in pairs], jnp.int32)
    ki_tab = jnp.asarray([b for _, b in pairs], jnp.int32)
    kern = functools.partial(_attn_kernel, bq=blk, bk=blk, hpb=hpb)
    grid_spec = pltpu.PrefetchScalarGridSpec(
        num_scalar_prefetch=2,
        grid=(batch, ATT_HEADS // hpb, len(pairs)),
        in_specs=[
            pl.BlockSpec(memory_space=pltpu.SMEM),
            pl.BlockSpec((1, blk, wb), lambda b, h, p, qt, kt: (0, b * nb + qt[p], h)),
            pl.BlockSpec((1, blk, wb), lambda b, h, p, qt, kt: (1, b * nb + kt[p], h)),
            pl.BlockSpec((1, blk, wb), lambda b, h, p, qt, kt: (2, b * nb + kt[p], h)),
            pl.BlockSpec((hpb, BLOCK, BLOCK), lambda b, h, p, qt, kt: (h, 0, 0)),
            pl.BlockSpec((hpb, BLOCK, BLOCK), lambda b, h, p, qt, kt: (h, 0, 0)),
            pl.BlockSpec((1, ATT_VD), lambda b, h, p, qt, kt: (0, 0)),
        ],
        out_specs=pl.BlockSpec((blk, wb), lambda b, h, p, qt, kt: (b * nb + qt[p], h)),
        scratch_shapes=[
            pltpu.VMEM((2 * hpb, blk, LANES), F32),
            pltpu.VMEM((2 * hpb, blk, LANES), F32),
            pltpu.VMEM((2 * hpb, blk, ATT_VD), F32),
        ],
    )
    return pl.pallas_call(
        kern,
        grid_spec=grid_spec,
        out_shape=jax.ShapeDtypeStruct((rp, d), BF16),
        compiler_params=pltpu.CompilerParams(
            dimension_semantics=("arbitrary", "arbitrary", "arbitrary"), vmem_limit_bytes=VMEM_LIMIT),
        name="diff_attn",
    )(qi_tab, ki_tab, lam, main, main, main, pat_d0, pat_d1, subln)


def _hgrn_kernel(q_ref, k_ref, v_ref, g_ref, gn_ref, o_ref, st_sc, b_sc, kf_sc, vf_sc, *, tc, hb):
    c_len = HG_CHUNK
    n_sub = HG_CHUNK // HG_SUB
    half = SUBLANES

    @pl.when(pl.program_id(2) == 0)
    def _():
        st_sc[...] = jnp.zeros(st_sc.shape, F32)

    ri = lax.broadcasted_iota(jnp.int32, (c_len, c_len), 0)
    ci = lax.broadcasted_iota(jnp.int32, (c_len, c_len), 1)
    tri = (ri >= ci).astype(F32)
    sub_r = ri // HG_SUB
    sub_c = ci // HG_SUB
    r2 = lax.broadcasted_iota(jnp.int32, (2 * LANES, 2 * LANES), 0) // LANES
    c2 = lax.broadcasted_iota(jnp.int32, (2 * LANES, 2 * LANES), 1) // LANES
    ones_bd = (r2 == c2).astype(BF16)
    row8 = lax.broadcasted_iota(jnp.int32, (half, LANES), 0)
    nt = (((1,), (1,)), ((), ()))

    def head(hh, r0, b_all):
        hs = slice(hh * HG_DK, (hh + 1) * HG_DK)
        q = q_ref[0, pl.ds(r0, c_len), hs].astype(F32)
        k = kf_sc[:, hs]
        v_bf = v_ref[0, pl.ds(r0, c_len), hs]
        b = b_all[:, hs]
        b_end = b[c_len - 1:c_len, :]
        st = st_sc[hh]

        o = lax.dot_general((q * jnp.exp2(b)).astype(BF16), st.astype(BF16), nt, preferred_element_type=F32)

        e_rows = [b[(j + 1) * HG_SUB - 1:(j + 1) * HG_SUB, :] for j in range(n_sub)]
        e_full = jnp.concatenate([jnp.broadcast_to(e, (HG_SUB, LANES)) for e in e_rows], axis=0)
        k_rel = (k * jnp.exp2(e_full - b)).astype(BF16)
        q_rel = jnp.concatenate(
            [(q * jnp.exp2(jnp.minimum(b - e_rows[j], 0.0))).astype(BF16) for j in range(n_sub - 1)], axis=0)
        a_all = lax.dot_general(q_rel, k_rel, nt, preferred_element_type=F32)
        a_off = jnp.zeros((c_len, c_len), F32)
        for j in range(n_sub - 1):
            a_off = jnp.where(sub_c == j, a_all[j * c_len:(j + 1) * c_len], a_off)
        a_off = jnp.where(sub_r > sub_c, a_off, 0.0)
        o = o + jnp.dot(a_off.astype(BF16), v_bf, preferred_element_type=F32)

        pieces = []
        meta = []
        for i in range(n_sub):
            for s in range(HG_SUB):
                row = i * HG_SUB + s
                ks = kf_sc[pl.ds(row, 1), hs]
                bs = b_sc[pl.ds(row, 1), hs]
                for hf in range(HG_SUB // half):
                    t0 = hf * half
                    if t0 + half - 1 < s:
                        continue
                    rows = slice(i * HG_SUB + t0, i * HG_SUB + t0 + half)
                    if s > t0:
                        w = q[rows] * ks * jnp.exp2(jnp.minimum(b[rows] - bs, 0.0))
                        w = jnp.where(row8 + t0 >= s, w, 0.0)
                    else:
                        w = q[rows] * ks * jnp.exp2(b[rows] - bs)
                    pieces.append(w.astype(BF16))
                    meta.append((i, s, hf))
        n_pairs = len(pieces) // 2
        lhs = jnp.concatenate(
            [jnp.concatenate([pieces[2 * n], pieces[2 * n + 1]], axis=1) for n in range(n_pairs)], axis=0)
        sums = jnp.dot(lhs, ones_bd, preferred_element_type=F32)
        diag = [[jnp.zeros((half, LANES), F32) for _ in range(HG_SUB // half)] for _ in range(n_sub)]
        for n, (i, s, hf) in enumerate(meta):
            blk = sums[(n // 2) * half:(n // 2 + 1) * half, (n % 2) * LANES:(n % 2 + 1) * LANES]
            vs = vf_sc[pl.ds(i * HG_SUB + s, 1), hs]
            diag[i][hf] = diag[i][hf] + blk * vs
        o = o + jnp.concatenate([d for row in diag for d in row], axis=0)

        k_out = (k * jnp.exp2(b_end - b)).astype(BF16)
        st_sc[hh] = st * jnp.exp2(b_end) + lax.dot_general(
            v_bf, k_out, (((0,), (0,)), ((), ())), preferred_element_type=F32)

        ms = jnp.mean(o * o, axis=-1, keepdims=True)
        o_ref[pl.ds(r0, c_len), hs] = (o * lax.rsqrt(ms + RMS_EPS) * gn_ref[...]).astype(BF16)

    def chunk(c, carry):
        r0 = pl.multiple_of(c * c_len, c_len)
        b_all = jnp.dot(tri, g_ref[pl.ds(r0, c_len), :], precision=lax.Precision.HIGHEST,
                        preferred_element_type=F32)
        b_sc[...] = b_all
        kf_sc[...] = k_ref[0, pl.ds(r0, c_len), :].astype(F32)
        vf_sc[...] = v_ref[0, pl.ds(r0, c_len), :].astype(F32)
        for hh in range(hb):
            head(hh, r0, b_all)
        return carry

    lax.fori_loop(0, tc // c_len, chunk, 0)


def _hgrn2(main, g, gn, batch, lp):
    _, rp, d = main.shape
    tc = _largest_divisor(lp, (640, 512, 256, 128, 64))
    nt_ = lp // tc
    hb = 16
    wb = hb * HG_DK
    kern = functools.partial(_hgrn_kernel, tc=tc, hb=hb)

    def seg_spec(seg):
        return pl.BlockSpec((1, tc, wb), lambda b, h, t: (seg, b * nt_ + t, h))

    return pl.pallas_call(
        kern,
        grid=(batch, HG_HEADS // hb, nt_),
        in_specs=[
            seg_spec(3), seg_spec(4), seg_spec(5),
            pl.BlockSpec((tc, wb), lambda b, h, t: (b * nt_ + t, h)),
            pl.BlockSpec((1, HG_DK), lambda b, h, t: (0, 0)),
        ],
        out_specs=pl.BlockSpec((tc, wb), lambda b, h, t: (b * nt_ + t, h)),
        out_shape=jax.ShapeDtypeStruct((rp, d), BF16),
        scratch_shapes=[
            pltpu.VMEM((hb, HG_DK, HG_DK), F32),
            pltpu.VMEM((HG_CHUNK, wb), F32),
            pltpu.VMEM((HG_CHUNK, wb), F32),
            pltpu.VMEM((HG_CHUNK, wb), F32),
        ],
        compiler_params=pltpu.CompilerParams(
            dimension_semantics=("arbitrary", "arbitrary", "arbitrary"), vmem_limit_bytes=VMEM_LIMIT),
        name="hgrn2",
    )(main, main, main, g, gn)


def _outproj_kernel(ga_ref, gh_ref, og_ref, oa_ref, oh_ref, h_ref, w_ref, n2_ref, rw_ref, rb_ref,
                    h1_ref, u2_ref, ti_ref, gt_ref):
    tm = h_ref.shape[0]
    y = (ga_ref[0].astype(F32) * oa_ref[...].astype(F32)
         + gh_ref[0].astype(F32) * (oh_ref[...].astype(F32) * og_ref[0].astype(F32)))
    h1 = h_ref[...] + jnp.dot(y.astype(BF16), w_ref[...], preferred_element_type=F32)
    h1_ref[...] = h1
    ms = jnp.mean(h1 * h1, axis=-1, keepdims=True)
    u2 = h1 * lax.rsqrt(ms + RMS_EPS) * n2_ref[...]
    for s in range(ROW_TILE):
        u2_ref[pl.ds(s, tm, stride=ROW_TILE), :] = u2[:, s * LANES:(s + 1) * LANES]
    logits = jnp.dot(u2, rw_ref[...], precision=lax.Precision.HIGHEST,
                     preferred_element_type=F32) + rb_ref[...]
    lane = lax.broadcasted_iota(jnp.int32, logits.shape, 1)
    cur = logits
    vals, idxs = [], []
    for _ in range(TOP_K):
        mx = jnp.max(cur, axis=-1, keepdims=True)
        ix = jnp.min(jnp.where(cur == mx, lane, LANES), axis=-1, keepdims=True)
        vals.append(mx)
        idxs.append(ix)
        cur = jnp.where(lane == ix, -jnp.inf, cur)
    es = [jnp.exp(v - vals[0]) for v in vals]
    inv = 1.0 / (es[0] + es[1] + es[2] + es[3])
    ti = jnp.zeros(logits.shape, jnp.int32)
    gt = jnp.zeros(logits.shape, F32)
    for kk in range(TOP_K):
        ti = jnp.where(lane == kk, idxs[kk], ti)
        gt = jnp.where(lane == kk, es[kk] * inv, gt)
    ti_ref[...] = ti
    gt_ref[...] = gt


def _out_proj(main, o_att, o_hg, h_pad, w_out_bf, n2, rw_pad, rb_pad):
    rp, d = h_pad.shape
    tm = _largest_divisor(rp, (256, 128))

    def seg_spec(seg):
        return pl.BlockSpec((1, tm, d), lambda i: (seg, i, 0))

    row = pl.BlockSpec((tm, d), lambda i: (i, 0))
    small = pl.BlockSpec((tm, LANES), lambda i: (i, 0))
    return pl.pallas_call(
        _outproj_kernel,
        grid=(rp // tm,),
        in_specs=[
            seg_spec(7), seg_spec(8), seg_spec(6), row, row, row,
            pl.BlockSpec((d, d), lambda i: (0, 0)),
            pl.BlockSpec((1, d), lambda i: (0, 0)),
            pl.BlockSpec((d, LANES), lambda i: (0, 0)),
            pl.BlockSpec((1, LANES), lambda i: (0, 0)),
        ],
        out_specs=[row, pl.BlockSpec((tm * ROW_TILE, LANES), lambda i: (i, 0)), small, small],
        out_shape=[
            jax.ShapeDtypeStruct((rp, d), F32),
            jax.ShapeDtypeStruct((rp * ROW_TILE, LANES), F32),
            jax.ShapeDtypeStruct((rp, LANES), jnp.int32),
            jax.ShapeDtypeStruct((rp, LANES), F32),
        ],
        compiler_params=pltpu.CompilerParams(
            dimension_semantics=("arbitrary",), vmem_limit_bytes=VMEM_LIMIT),
        name="out_proj",
    )(main, main, main, o_att, o_hg, h_pad, w_out_bf, n2, rw_pad, rb_pad)


def _rank_kernel(ti_ref, rank_ref, cnt_ref, carry_sc, *, tm, lp):
    i = pl.program_id(0)

    @pl.when(i == 0)
    def _():
        carry_sc[...] = jnp.zeros(carry_sc.shape, F32)

    ti = ti_ref[...]
    lane = lax.broadcasted_iota(jnp.int32, (tm, LANES), 1)
    row = i * tm + lax.broadcasted_iota(jnp.int32, (tm, 1), 0)
    valid = lax.rem(row, lp) >= PAD
    hot = [(lane == ti[:, kk:kk + 1]) & valid for kk in range(TOP_K)]
    any_hot = hot[0] | hot[1] | hot[2] | hot[3]
    any_f = jnp.where(any_hot, 1.0, 0.0)
    rr = lax.broadcasted_iota(jnp.int32, (tm, tm), 0)
    cc = lax.broadcasted_iota(jnp.int32, (tm, tm), 1)
    strict = jnp.where(rr > cc, 1.0, 0.0).astype(BF16)
    base = carry_sc[...] + jnp.dot(strict, any_f.astype(BF16), preferred_element_type=F32)
    rank = jnp.zeros((tm, LANES), F32)
    for kk in range(TOP_K):
        rk = jnp.sum(jnp.where(hot[kk], base, 0.0), axis=-1, keepdims=True)
        rank = jnp.where(lane == kk, rk, rank)
    rank_ref[...] = rank.astype(jnp.int32)
    carry_sc[...] = carry_sc[...] + jnp.sum(any_f, axis=0, keepdims=True)
    cnt_ref[...] = carry_sc[...]


def _moe_rank(topi, lp):
    rp = topi.shape[0]
    tm = _largest_divisor(rp, (256, 128))
    kern = functools.partial(_rank_kernel, tm=tm, lp=lp)
    return pl.pallas_call(
        kern,
        grid=(rp // tm,),
        in_specs=[pl.BlockSpec((tm, LANES), lambda i: (i, 0))],
        out_specs=[pl.BlockSpec((tm, LANES), lambda i: (i, 0)), pl.BlockSpec((1, LANES), lambda i: (0, 0))],
        out_shape=[jax.ShapeDtypeStruct((rp, LANES), jnp.int32), jax.ShapeDtypeStruct((1, LANES), F32)],
        scratch_shapes=[pltpu.VMEM((1, LANES), F32)],
        compiler_params=pltpu.CompilerParams(dimension_semantics=("arbitrary",)),
        name="moe_rank",
    )(topi)


def _row_gather_start(src_hbm, dst, sem, tok, r):
    pltpu.make_async_copy(
        src_hbm.at[pl.ds(pl.multiple_of(tok * ROW_TILE, ROW_TILE), ROW_TILE), :],
        dst.at[pl.ds(pl.multiple_of(r * ROW_TILE, ROW_TILE), ROW_TILE), :],
        sem).start()


def _ffn_kernel(bexp_ref, nact_ref, tokc_ref, tokn_ref, u2_hbm, wgu_hbm, wdn_hbm, bgu_ref, bdn_ref, ys_ref,
                gbuf, ring, gu_sc, acc_sc, gsem, wsem, *, rb, d, dff, tk):
    i = pl.program_id(0)
    nact = nact_ref[0]
    slot = lax.rem(i, 2)
    nka = d // tk
    nkb = dff // tk
    n_chunks = 2 * nka + nkb
    n_ring = ring.shape[0]
    assert n_chunks % n_ring == 0 and dff == d

    def chunk_copy(blk, c, ring_slot):
        e = bexp_ref[blk]
        if c < 2 * nka:
            kt, half = divmod(c, 2)
            src = wgu_hbm.at[e, pl.ds(kt * tk, tk), pl.ds(half * dff, dff)]
        else:
            src = wdn_hbm.at[e, pl.ds((c - 2 * nka) * tk, tk), :]
        return pltpu.make_async_copy(src, ring.at[ring_slot], wsem.at[ring_slot])

    def issue_rows(tok_ref, sl):
        def body(r, carry):
            _row_gather_start(u2_hbm, gbuf.at[sl], gsem.at[sl], tok_ref[0, 0, r], r)
            return carry
        lax.fori_loop(0, rb, body, 0, unroll=8)

    @pl.when(i == 0)
    def _():
        issue_rows(tokc_ref, 0)
        for c in range(n_ring - 1):
            chunk_copy(0, c, c).start()

    @pl.when(i + 1 < nact)
    def _():
        issue_rows(tokn_ref, 1 - slot)

    @pl.when(i < nact)
    def _():
        pltpu.make_async_copy(gbuf.at[slot], gbuf.at[slot], gsem.at[slot]).wait()
        gu_sc[...] = jnp.broadcast_to(bgu_ref[0], gu_sc.shape)
        acc_sc[...] = jnp.broadcast_to(bdn_ref[0], acc_sc.shape)
        xk = None
        for c in range(n_chunks):
            rs = c % n_ring
            chunk_copy(i, c, rs).wait()
            nxt = c + n_ring - 1
            if nxt < n_chunks:
                chunk_copy(i, nxt, nxt % n_ring).start()
            else:
                chunk_copy(i + 1, nxt - n_chunks, nxt % n_ring).start()
            w = ring[rs].astype(BF16)
            if c < 2 * nka:
                kt, half = divmod(c, 2)
                if half == 0:
                    per = tk // LANES
                    xk = jnp.concatenate(
                        [gbuf[slot, pl.ds(kt * per + s, rb, stride=ROW_TILE), :].astype(BF16)
                         for s in range(per)], axis=1)
                cols = slice(half * dff, (half + 1) * dff)
                gu_sc[:, cols] += jnp.dot(xk, w, preferred_element_type=F32)
            else:
                kt = c - 2 * nka
                xg = jnp.minimum(gu_sc[:, kt * tk:(kt + 1) * tk], SWIGLU_LIMIT)
                xl = jnp.clip(gu_sc[:, dff + kt * tk:dff + (kt + 1) * tk], -SWIGLU_LIMIT, SWIGLU_LIMIT)
                act = (xg * _sigmoid(SWIGLU_ALPHA * xg) * (xl + 1.0)).astype(BF16)
                acc_sc[...] += jnp.dot(act, w, preferred_element_type=F32)
        for s in range(ROW_TILE):
            ys_ref[pl.ds(s, rb, stride=ROW_TILE), :] = acc_sc[:, s * LANES:(s + 1) * LANES]

    @pl.when(i == nact)
    def _():
        for c in range(n_ring - 1):
            chunk_copy(i, c, c).wait()

    @pl.when(i >= nact)
    def _():
        ys_ref[...] = jnp.zeros(ys_ref.shape, F32)


def _moe_ffn(block_exp, nact, slot_tok, u2_rows, w_gu, b_gu, w_dn, b_dn, rb):
    n_e, d, f2 = w_gu.shape
    dff = f2 // 2
    nblk = slot_tok.shape[0]
    tk = 512
    n_ring = 4
    kern = functools.partial(_ffn_kernel, rb=rb, d=d, dff=dff, tk=tk)
    grid_spec = pltpu.PrefetchScalarGridSpec(
        num_scalar_prefetch=2,
        grid=(nblk,),
        in_specs=[
            pl.BlockSpec((1, 1, rb), lambda i, be, na: (i, 0, 0), memory_space=pltpu.SMEM),
            pl.BlockSpec((1, 1, rb), lambda i, be, na: (jnp.minimum(i + 1, nblk - 1), 0, 0),
                         memory_space=pltpu.SMEM),
            pl.BlockSpec(memory_space=pl.ANY),
            pl.BlockSpec(memory_space=pl.ANY),
            pl.BlockSpec(memory_space=pl.ANY),
            pl.BlockSpec((1, 1, f2), lambda i, be, na: (be[i], 0, 0)),
            pl.BlockSpec((1, 1, d), lambda i, be, na: (be[i], 0, 0)),
        ],
        out_specs=pl.BlockSpec((rb * ROW_TILE, LANES), lambda i, be, na: (i, 0)),
        scratch_shapes=[
            pltpu.VMEM((2, rb * ROW_TILE, LANES), F32),
            pltpu.VMEM((n_ring, tk, d), F32),
            pltpu.VMEM((rb, f2), F32),
            pltpu.VMEM((rb, d), F32),
            pltpu.SemaphoreType.DMA((2,)),
            pltpu.SemaphoreType.DMA((n_ring,)),
        ],
    )
    return pl.pallas_call(
        kern,
        grid_spec=grid_spec,
        out_shape=jax.ShapeDtypeStruct((nblk * rb * ROW_TILE, LANES), F32),
        compiler_params=pltpu.CompilerParams(
            dimension_semantics=("arbitrary",), vmem_limit_bytes=VMEM_LIMIT),
        name="moe_ffn",
    )(block_exp, nact, slot_tok, slot_tok, u2_rows, w_gu, w_dn, b_gu.reshape(n_e, 1, f2),
      b_dn.reshape(n_e, 1, d))


def _combine_kernel(dc_ref, dn_ref, ys_hbm, gt_ref, h1_ref, o_ref, cbuf, sem, *, tm, n_steps, n_i):
    n = pl.program_id(0) * n_i + pl.program_id(1)
    slot = lax.rem(n, 2)

    def issue(dref, sl):
        def body(r, carry):
            for kk in range(TOP_K):
                _row_gather_start(ys_hbm, cbuf.at[sl, kk], sem.at[sl], dref[0, 0, r * TOP_K + kk], r)
            return carry
        lax.fori_loop(0, tm, body, 0, unroll=4)

    @pl.when(n == 0)
    def _():
        issue(dc_ref, 0)

    @pl.when(n + 1 < n_steps)
    def _():
        issue(dn_ref, 1 - slot)

    pltpu.make_async_copy(cbuf.at[slot], cbuf.at[slot], sem.at[slot]).wait()
    gates = [jnp.broadcast_to(gt_ref[:, kk:kk + 1], (tm, LANES)) for kk in range(TOP_K)]
    for s in range(ROW_TILE):
        sl = slice(s * LANES, (s + 1) * LANES)
        y = h1_ref[:, sl]
        for kk in range(TOP_K):
            y = y + gates[kk] * cbuf[slot, kk, pl.ds(s, tm, stride=ROW_TILE), :]
        o_ref[0, :, sl] = y


def _combine(dest_x, ys_rows, gate, h1, batch, seq, lp):
    rp, d = h1.shape
    tm = BLOCK
    n_i = seq // tm
    n_steps = batch * n_i
    nlp = lp // tm
    kern = functools.partial(_combine_kernel, tm=tm, n_steps=n_steps, n_i=n_i)
    return pl.pallas_call(
        kern,
        grid=(batch, n_i),
        in_specs=[
            pl.BlockSpec((1, 1, tm * TOP_K), lambda b, i: (b * n_i + i, 0, 0), memory_space=pltpu.SMEM),
            pl.BlockSpec((1, 1, tm * TOP_K), lambda b, i: (jnp.minimum(b * n_i + i + 1, n_steps - 1), 0, 0),
                         memory_space=pltpu.SMEM),
            pl.BlockSpec(memory_space=pl.ANY),
            pl.BlockSpec((tm, LANES), lambda b, i: (b * nlp + 1 + i, 0)),
            pl.BlockSpec((tm, d), lambda b, i: (b * nlp + 1 + i, 0)),
        ],
        out_specs=pl.BlockSpec((1, tm, d), lambda b, i: (b, i, 0)),
        out_shape=jax.ShapeDtypeStruct((batch, seq, d), F32),
        scratch_shapes=[
            pltpu.VMEM((2, TOP_K, tm * ROW_TILE, LANES), F32),
            pltpu.SemaphoreType.DMA((2,)),
        ],
        compiler_params=pltpu.CompilerParams(
            dimension_semantics=("arbitrary", "arbitrary"), vmem_limit_bytes=VMEM_LIMIT),
        name="moe_combine",
    )(dest_x, dest_x, ys_rows, gate, h1)


def kernel(x, meta_tokens, rel_bias, lb_logits, norm1, w_in, q_norm, k_norm, diff_lambda, diff_subln,
           hgrn_norm, w_out, norm2, router_w, router_b, w_gate_up, b_gate_up, w_down, b_down):
    batch, seq, d = x.shape
    assert norm1.shape[0] == 1, "single-layer block"
    assert d == ATT_HEADS * ATT_VD == HG_HEADS * HG_DK == ROW_TILE * LANES and seq % BLOCK == 0
    lp = seq + BLOCK
    rp = batch * lp

    first = jnp.concatenate([jnp.zeros((PAD, d), x.dtype), meta_tokens.astype(x.dtype)], axis=0)
    h_pad = jnp.concatenate([jnp.broadcast_to(first[None], (batch, BLOCK, d)), x], axis=1).reshape(rp, d)

    lb = jax.nn.softmax(lb_logits.astype(F32), axis=0)[0]
    lv = diff_lambda[0].astype(F32)
    lam = (jnp.exp(jnp.sum(lv[0] * lv[1])) - jnp.exp(jnp.sum(lv[2] * lv[3])) + LAMBDA_INIT).reshape(1, 1)
    n_grp = d // ATT_HD
    gains = jnp.concatenate([jnp.tile(q_norm[0].astype(F32) * (ATT_HD ** -0.5 * LOG2E), n_grp),
                             jnp.tile(k_norm[0].astype(F32), n_grp),
                             jnp.ones(((N_SEG - 2) * d,), F32)])
    lbs = jnp.concatenate([jnp.zeros((4 * d,), F32), lb, jnp.zeros((4 * d,), F32)])
    colp = jnp.stack([gains, lbs])

    main, g = _in_proj(h_pad, norm1.astype(F32), w_in[0].astype(BF16), colp)

    blk = _largest_divisor(lp, (640, 512, 384, 256, 128))
    pat_d0, pat_d1 = _bias_patterns(rel_bias)
    subln = (diff_subln[0].astype(F32) * (1.0 - LAMBDA_INIT)).reshape(1, ATT_VD)
    o_att = _diff_attention(main, lam, pat_d0, pat_d1, subln, batch, lp, blk)

    o_hg = _hgrn2(main, g, hgrn_norm.astype(F32).reshape(1, HG_DK), batch, lp)

    rw_pad = jnp.pad(router_w[0].astype(F32), ((0, 0), (0, LANES - N_EXPERTS)))
    rb_pad = jnp.pad(router_b[0].astype(F32), (0, LANES - N_EXPERTS), constant_values=NEG_INF).reshape(1, LANES)
    h1, u2_rows, topi, gate = _out_proj(main, o_att, o_hg, h_pad, w_out[0].astype(BF16), norm2.astype(F32),
                                        rw_pad, rb_pad)

    rank, counts = _moe_rank(topi, lp)

    rb = 512
    n_assign = batch * (seq + N_META) * TOP_K
    nblk = n_assign // rb + N_EXPERTS + 1
    cnt = counts[0, :N_EXPERTS].astype(jnp.int32)
    padded = (cnt + rb - 1) // rb * rb
    pend = jnp.cumsum(padded)
    pstart = pend - padded
    rows = jnp.arange(rp, dtype=jnp.int32)
    valid = (rows % lp) >= PAD
    ti4 = topi[:, :TOP_K]
    dest = pstart[ti4] + rank[:, :TOP_K]
    oob = nblk * rb + rows[:, None] * TOP_K + jnp.arange(TOP_K, dtype=jnp.int32)[None, :]
    dest_s = jnp.where(valid[:, None], dest, oob)
    slot_tok = jnp.zeros((nblk * rb,), jnp.int32).at[dest_s.reshape(-1)].set(
        jnp.repeat(rows, TOP_K), mode="drop", unique_indices=True)
    nact = (pend[-1] // rb).astype(jnp.int32)
    blk_ids = jnp.minimum(jnp.arange(nblk, dtype=jnp.int32), nact - 1)
    block_exp = jnp.minimum(jnp.sum((pend[None, :] <= (blk_ids * rb)[:, None]).astype(jnp.int32), axis=1),
                            N_EXPERTS - 1)

    ys_rows = _moe_ffn(block_exp, nact.reshape(1), slot_tok.reshape(nblk, 1, rb), u2_rows,
                       w_gate_up[0], b_gate_up[0].astype(F32), w_down[0], b_down[0].astype(F32), rb)

    dest_x = dest.reshape(batch, lp, TOP_K)[:, BLOCK:, :].reshape(batch * seq // BLOCK, 1, BLOCK * TOP_K)
    return _combine(dest_x, ys_rows, gate, h1, batch, seq, lp)
```
